```python
import jax, jax.numpy as jnp
from jax import lax
import numpy as np

D_MODEL = 4096
BATCH = 32
SEQ = 256
DEPTH = 2
DEC_BATCH = 2
DEC_SEQ = 4096
PAST_LEN = 256

GRID_W = 64
HEAD_DIM = 128
ROPE_THETA = 10000.0
NORM_EPS = 1e-6
Q_BLOCK = 128
A_HEADS = 24
A_KV_HEADS = 6
A_GROUP = A_HEADS // A_KV_HEADS
B_GROUPS = 8
B_GROUP_W = 128
A_Q_W = A_HEADS * HEAD_DIM
A_KV_W = A_KV_HEADS * HEAD_DIM
B_W = B_GROUPS * B_GROUP_W
EVEN_IN_W = A_Q_W + 2 * A_KV_W + B_W
EVEN_MIX_W = A_Q_W + B_W
C_HEADS = 24
Q_LORA = 896
KV_LORA = 512
QK_NOPE = 128
QK_ROPE = 64
V_HEAD = 128
D_CH = 1024
CONV_W = 3
ODD_IN_W = Q_LORA + KV_LORA + QK_ROPE + 3 * D_CH
ODD_MIX_W = C_HEADS * V_HEAD + D_CH
N_EXPERTS = 32
TOP_K = 4
EXPERT_FF = 1024
SWIGLU_LIMIT = 7.0
SWIGLU_ALPHA = 1.702
N_MOD = 6
N_EVEN = (DEPTH + 1) // 2
N_ODD = DEPTH // 2

kernel_name = "hybrid_diffusion_prefix_trunk_step"


def rmsnorm(x, w):
    xf = x.astype(jnp.float32)
    y = xf * lax.rsqrt(jnp.mean(xf * xf, axis=-1, keepdims=True) + NORM_EPS)
    return (y * w.astype(jnp.float32)).astype(x.dtype)


def axial_angles(n_tokens, rot_dim):
    rows = n_tokens // GRID_W
    row = jnp.repeat(jnp.arange(rows), GRID_W).astype(jnp.float32)
    col = jnp.tile(jnp.arange(GRID_W), rows).astype(jnp.float32)
    half = rot_dim // 2
    inv = ROPE_THETA ** (-jnp.arange(0, half, 2, dtype=jnp.float32) / half)
    return row[:, None] * inv[None, :], col[:, None] * inv[None, :]


def rope_rotate(x, ang):
    a = jnp.concatenate([ang, ang], axis=-1)[None, :, None, :]
    d2 = x.shape[-1] // 2
    xf = x.astype(jnp.float32)
    rot = jnp.concatenate([-xf[..., d2:], xf[..., :d2]], axis=-1)
    return (xf * jnp.cos(a) + rot * jnp.sin(a)).astype(x.dtype)


def apply_axial_rope(x, ang_r, ang_c):
    half = x.shape[-1] // 2
    return jnp.concatenate([rope_rotate(x[..., :half], ang_r), rope_rotate(x[..., half:], ang_c)], axis=-1)


def block_attention(q, k, v, scale):
    b, sq, nkv, g, dk = q.shape
    nb = sq // Q_BLOCK
    qb = q.reshape(b, nb, Q_BLOCK, nkv, g, dk).transpose(1, 0, 2, 3, 4, 5)

    def one_block(qblk):
        s = jnp.einsum('bqhgd,bkhd->bhgqk', qblk, k).astype(jnp.float32) * scale
        p = jax.nn.softmax(s, axis=-1).astype(v.dtype)
        return jnp.einsum('bhgqk,bkhe->bqhge', p, v)

    o = lax.map(one_block, qb)
    return o.transpose(1, 0, 2, 3, 4, 5).reshape(b, sq, nkv * g * v.shape[-1])


def short_conv_centred(z, w):
    s = z.shape[1]
    pad = CONV_W // 2
    zp = jnp.pad(z, ((0, 0), (pad, pad), (0, 0)))
    return sum(zp[:, j:j + s] * w[j] for j in range(CONV_W))


def even_mixer(h, w_in, q_norm_w, k_norm_w, w_out, ctx_k=None, ctx_v=None):
    b, s, _ = h.shape
    proj = h @ w_in
    q, k, v, f = jnp.split(proj, [A_Q_W, A_Q_W + A_KV_W, A_Q_W + 2 * A_KV_W], axis=-1)
    q = rmsnorm(q.reshape(b, s, A_HEADS, HEAD_DIM), q_norm_w)
    k = rmsnorm(k.reshape(b, s, A_KV_HEADS, HEAD_DIM), k_norm_w)
    v = v.reshape(b, s, A_KV_HEADS, HEAD_DIM)
    if ctx_k is None:
        keys, vals, state = k, v, (k, v)
    else:
        ang_r, ang_c = axial_angles(s, HEAD_DIM)
        q = apply_axial_rope(q, ang_r, ang_c)
        k = apply_axial_rope(k, ang_r, ang_c)
        keys = jnp.concatenate([ctx_k.astype(k.dtype), k], axis=1)
        vals = jnp.concatenate([ctx_v.astype(v.dtype), v], axis=1)
        state = None
    attn = block_attention(q.reshape(b, s, A_KV_HEADS, A_GROUP, HEAD_DIM), keys, vals, HEAD_DIM ** -0.5)
    fg = f.reshape(b, s, B_GROUPS, B_GROUP_W).astype(jnp.float32)
    four = jnp.fft.fft2(fg, axes=(1, 3), norm='ortho').real.astype(h.dtype).reshape(b, s, B_W)
    out = jnp.concatenate([attn, four], axis=-1) @ w_out
    return out, state


def odd_mixer(h, w_in, q_a_norm_w, kv_a_norm_w, w_uq, w_ukv, conv_w, w_out, ctx_ckv=None, ctx_krope=None):
    b, s, _ = h.shape
    proj = h @ w_in
    o1 = Q_LORA
    o2 = o1 + KV_LORA
    o3 = o2 + QK_ROPE
    o4 = o3 + D_CH
    o5 = o4 + D_CH
    cq, ckv, krope, u, gb, gc = jnp.split(proj, [o1, o2, o3, o4, o5], axis=-1)
    q = (rmsnorm(cq, q_a_norm_w) @ w_uq).reshape(b, s, C_HEADS, QK_NOPE + QK_ROPE)
    q_nope, q_rope = q[..., :QK_NOPE], q[..., QK_NOPE:]
    ckv = rmsnorm(ckv, kv_a_norm_w)
    if ctx_ckv is None:
        all_ckv, all_kr, state = ckv, krope, (ckv, krope)
    else:
        ang_r, ang_c = axial_angles(s, QK_ROPE)
        q_rope = apply_axial_rope(q_rope, ang_r, ang_c)
        kr = apply_axial_rope(krope[:, :, None, :], ang_r, ang_c)[:, :, 0, :]
        all_ckv = jnp.concatenate([ctx_ckv.astype(ckv.dtype), ckv], axis=1)
        all_kr = jnp.concatenate([ctx_krope.astype(kr.dtype), kr], axis=1)
        state = None
    nk = all_ckv.shape[1]
    kv = (all_ckv @ w_ukv).reshape(b, nk, C_HEADS, QK_NOPE + V_HEAD)
    k_nope, v = kv[..., :QK_NOPE], kv[..., QK_NOPE:]
    k = jnp.concatenate([k_nope, jnp.broadcast_to(all_kr[:, :, None, :], (b, nk, C_HEADS, QK_ROPE))], axis=-1)
    qf = jnp.concatenate([q_nope, q_rope], axis=-1)[:, :, :, None, :]
    attn = block_attention(qf, k, v, (QK_NOPE + QK_ROPE) ** -0.5)
    conv = gb * short_conv_centred(gc * u, conv_w)
    out = jnp.concatenate([attn, conv], axis=-1) @ w_out
    return out, state


def moe(h, router_w, router_b, w_gate, b_gate, w_up, b_up, w_down, b_down):
    b, s, d = h.shape
    x = h.reshape(b * s, d)
    logits = (x @ router_w + router_b).astype(jnp.float32)
    top_v, top_i = lax.top_k(logits, TOP_K)
    wts = jax.nn.softmax(top_v, axis=-1)
    gates = jnp.sum(jax.nn.one_hot(top_i, N_EXPERTS, dtype=jnp.float32) * wts[..., None], axis=-2).astype(x.dtype)
    out = jnp.zeros_like(x)
    for e in range(N_EXPERTS):
        g = jnp.minimum(x @ w_gate[e] + b_gate[e], SWIGLU_LIMIT)
        up = jnp.clip(x @ w_up[e] + b_up[e], -SWIGLU_LIMIT, SWIGLU_LIMIT)
        y = ((up + 1.0) * g * jax.nn.sigmoid(SWIGLU_ALPHA * g)) @ w_down[e] + b_down[e]
        out = out + gates[:, e:e + 1] * y
    return out.reshape(b, s, d)


def setup_inputs(seed: int = 0) -> dict:
    key = jax.random.key(seed)
    ks = jax.random.split(key, 36)
    nrm = lambda k, shape, scale: jax.random.normal(k, shape, jnp.float32) * scale
    gain = lambda k, shape: 1.0 + 0.01 * jax.random.normal(k, shape, jnp.float32)
    D = D_MODEL
    return {
        "x_prompt": nrm(ks[0], (BATCH, SEQ, D), 1.0),
        "x_sample": nrm(ks[1], (DEC_BATCH, DEC_SEQ, D), 1.0),
        "cache_attn_k": nrm(ks[2], (DEC_BATCH, N_EVEN, PAST_LEN, A_KV_HEADS, HEAD_DIM), 1.0),
        "cache_attn_v": nrm(ks[3], (DEC_BATCH, N_EVEN, PAST_LEN, A_KV_HEADS, HEAD_DIM), 1.0),
        "cache_mla_ckv": nrm(ks[4], (DEC_BATCH, N_ODD, PAST_LEN, KV_LORA), 1.0),
        "cache_mla_krope": nrm(ks[5], (DEC_BATCH, N_ODD, PAST_LEN, QK_ROPE), 1.0),
        "c": nrm(ks[6], (DEC_BATCH, D), 1.0),
        "c_ctx": nrm(ks[7], (D,), 1.0),
        "norm1_w": gain(ks[8], (DEPTH, D)),
        "norm2_w": gain(ks[9], (DEPTH, D)),
        "w_ada": nrm(ks[10], (DEPTH, D, N_MOD * D), 0.5 * D ** -0.5),
        "b_ada": nrm(ks[11], (DEPTH, N_MOD * D), 0.01),
        "even_w_in": nrm(ks[12], (N_EVEN, D, EVEN_IN_W), D ** -0.5),
        "even_q_norm": gain(ks[13], (N_EVEN, HEAD_DIM)),
        "even_k_norm": gain(ks[14], (N_EVEN, HEAD_DIM)),
        "even_w_out": nrm(ks[15], (N_EVEN, EVEN_MIX_W, D), EVEN_MIX_W ** -0.5),
        "odd_w_in": nrm(ks[16], (N_ODD, D, ODD_IN_W), D ** -0.5),
        "odd_q_a_norm": gain(ks[17], (N_ODD, Q_LORA)),
        "odd_kv_a_norm": gain(ks[18], (N_ODD, KV_LORA)),
        "odd_w_uq": nrm(ks[19], (N_ODD, Q_LORA, C_HEADS * (QK_NOPE + QK_ROPE)), Q_LORA ** -0.5),
        "odd_w_ukv": nrm(ks[20], (N_ODD, KV_LORA, C_HEADS * (QK_NOPE + V_HEAD)), KV_LORA ** -0.5),
        "odd_conv_w": nrm(ks[21], (N_ODD, CONV_W, D_CH), CONV_W ** -0.5),
        "odd_w_out": nrm(ks[22], (N_ODD, ODD_MIX_W, D), ODD_MIX_W ** -0.5),
        "router_w": nrm(ks[23], (DEPTH, D, N_EXPERTS), D ** -0.5),
        "router_b": nrm(ks[24], (DEPTH, N_EXPERTS), 0.01),
        "w_gate": nrm(ks[25], (DEPTH, N_EXPERTS, D, EXPERT_FF), D ** -0.5),
        "b_gate": nrm(ks[26], (DEPTH, N_EXPERTS, EXPERT_FF), 0.01),
        "w_up": nrm(ks[27], (DEPTH, N_EXPERTS, D, EXPERT_FF), D ** -0.5),
        "b_up": nrm(ks[28], (DEPTH, N_EXPERTS, EXPERT_FF), 0.01),
        "w_down": nrm(ks[29], (DEPTH, N_EXPERTS, EXPERT_FF, D), EXPERT_FF ** -0.5),
        "b_down": nrm(ks[30], (DEPTH, N_EXPERTS, D), 0.01),
        "final_norm_w": gain(ks[31], (D,)),
    }


def reference(x_prompt, x_sample, cache_attn_k, cache_attn_v, cache_mla_ckv, cache_mla_krope, c, c_ctx,
              norm1_w, norm2_w, w_ada, b_ada,
              even_w_in, even_q_norm, even_k_norm, even_w_out,
              odd_w_in, odd_q_a_norm, odd_kv_a_norm, odd_w_uq, odd_w_ukv, odd_conv_w, odd_w_out,
              router_w, router_b, w_gate, b_gate, w_up, b_up, w_down, b_down, final_norm_w):
    cond_p = jax.nn.silu(c_ctx)[None, :]
    cond_s = jax.nn.silu(c)
    xp, xs = x_prompt, x_sample
    st_k, st_v, st_ckv, st_kr = [], [], [], []
    for l in range(DEPTH):
        mod_p = jnp.split((cond_p @ w_ada[l] + b_ada[l])[:, None, :], N_MOD, axis=-1)
        mod_s = jnp.split((cond_s @ w_ada[l] + b_ada[l])[:, None, :], N_MOD, axis=-1)
        hp = rmsnorm(xp, norm1_w[l]) * (1.0 + mod_p[1]) + mod_p[0]
        hs = rmsnorm(xs, norm1_w[l]) * (1.0 + mod_s[1]) + mod_s[0]
        j = l // 2
        if l % 2 == 0:
            prm = (even_w_in[j], even_q_norm[j], even_k_norm[j], even_w_out[j])
            yp, (kp, vp) = even_mixer(hp, *prm)
            ys, _ = even_mixer(hs, *prm, ctx_k=cache_attn_k[:, j], ctx_v=cache_attn_v[:, j])
            st_k.append(kp)
            st_v.append(vp)
        else:
            prm = (odd_w_in[j], odd_q_a_norm[j], odd_kv_a_norm[j], odd_w_uq[j], odd_w_ukv[j], odd_conv_w[j], odd_w_out[j])
            yp, (ckvp, krp) = odd_mixer(hp, *prm)
            ys, _ = odd_mixer(hs, *prm, ctx_ckv=cache_mla_ckv[:, j], ctx_krope=cache_mla_krope[:, j])
            st_ckv.append(ckvp)
            st_kr.append(krp)
        xp = xp + mod_p[2] * yp
        xs = xs + mod_s[2] * ys
        moe_prm = (router_w[l], router_b[l], w_gate[l], b_gate[l], w_up[l], b_up[l], w_down[l], b_down[l])
        hp = rmsnorm(xp, norm2_w[l]) * (1.0 + mod_p[4]) + mod_p[3]
        hs = rmsnorm(xs, norm2_w[l]) * (1.0 + mod_s[4]) + mod_s[3]
        xp = xp + mod_p[5] * moe(hp, *moe_prm)
        xs = xs + mod_s[5] * moe(hs, *moe_prm)
    y_prompt = rmsnorm(xp, final_norm_w)
    y_sample = rmsnorm(xs, final_norm_w)
    state_attn_k = jnp.stack(st_k, axis=1)
    state_attn_v = jnp.stack(st_v, axis=1)
    state_mla_ckv = jnp.stack(st_ckv, axis=1)
    state_mla_krope = jnp.stack(st_kr, axis=1)
    return (y_prompt, y_sample, state_attn_k, state_attn_v, state_mla_ckv, state_mla_krope)
```

```python
import functools
import math

import jax
import jax.numpy as jnp
from jax import lax
from jax.experimental import pallas as pl
from jax.experimental.pallas import tpu as pltpu

F32 = jnp.float32
BF16 = jnp.bfloat16
I32 = jnp.int32

GRID_W = 64
HEAD_DIM = 128
ROPE_THETA = 10000.0
NORM_EPS = 1e-6
A_HEADS = 24
A_KV_HEADS = 6
B_GROUPS = 8
B_GROUP_W = 128
C_HEADS = 24
Q_LORA = 896
KV_LORA = 512
QK_NOPE = 128
QK_ROPE = 64
V_HEAD = 128
D_CH = 1024
CONV_W = 3
N_EXPERTS = 32
TOP_K = 4
EXPERT_FF = 1024
SWIGLU_LIMIT = 7.0
SWIGLU_ALPHA = 1.702
N_MOD = 6

LANE = 128
SUBLANE = 8
V7X_VMEM_BYTES = 64 * 2**20
MIB = 2**20

ROW_TILE = 512
COL_TILE = 512
POST_ROW_TILE = 256
ATTN_Q_TILE = 256
ADA_COL_TILE = 512
EXPERT_ROW_TILE = 512
GATE_UP_COL_TILE = 256
DOWN_COL_TILE = 2048
GATHER_ROW_TILE = 256
COMBINE_ROW_TILE = 128


def _tile(n, pref, align=SUBLANE):
    t = min(pref, n)
    t -= t % align
    while t > align and n % t:
        t -= align
    assert t >= align and n % t == 0, (n, pref, align)
    return t


def _params(sem, vmem_bytes):
    limit = int(min(max(vmem_bytes, 16 * MIB), V7X_VMEM_BYTES - 6 * MIB))
    return pltpu.CompilerParams(dimension_semantics=sem, vmem_limit_bytes=limit)


def _group_of_row(start, rows_p, seq_s):
    return jnp.where(start < rows_p, 0, 1 + jnp.maximum(start - rows_p, 0) // seq_s)


def _ada_kernel(c_ref, w_ref, b_ref, o_ref):
    c = c_ref[...]
    s = (c * jax.nn.sigmoid(c)).astype(BF16)
    o_ref[...] = jnp.dot(s, w_ref[...].astype(BF16), preferred_element_type=F32) + b_ref[...]


def _ada_modulation(cond, w_ada, b_ada):
    depth, d, n = w_ada.shape
    g8 = cond.shape[0]
    tn = _tile(n, ADA_COL_TILE, LANE)
    vmem = 2 * (d * tn * 4) + d * tn * 2 + 4 * MIB
    return pl.pallas_call(
        _ada_kernel,
        grid=(depth, n // tn),
        in_specs=[
            pl.BlockSpec((g8, d), lambda l, j: (0, 0)),
            pl.BlockSpec((None, d, tn), lambda l, j: (l, 0, j)),
            pl.BlockSpec((None, 1, tn), lambda l, j: (l, 0, j)),
        ],
        out_specs=pl.BlockSpec((None, g8, tn), lambda l, j: (l, 0, j)),
        out_shape=jax.ShapeDtypeStruct((depth, g8, n), F32),
        compiler_params=_params(("parallel", "parallel"), vmem),
        name="ada_modulation",
    )(cond, w_ada, b_ada.reshape(depth, 1, n))


def _rmsnorm(x, w):
    return x * lax.rsqrt(jnp.mean(x * x, axis=-1, keepdims=True) + NORM_EPS) * w


def _rope_lanes(y, cos, sin, half):
    lane = lax.broadcasted_iota(I32, y.shape, 1)
    partner = jnp.where(lane % (2 * half) < half,
                        pltpu.roll(y, LANE - half, 1),
                        pltpu.roll(y, half, 1))
    return y * cos + partner * sin


def _rope_tables(n_tokens, rot_dim, ident_rows):
    rows = n_tokens // GRID_W
    row = jnp.repeat(jnp.arange(rows), GRID_W).astype(F32)
    col = jnp.tile(jnp.arange(GRID_W), rows).astype(F32)
    half = rot_dim // 2
    inv = ROPE_THETA ** (-jnp.arange(0, half, 2, dtype=F32) / half)
    ang_r = row[:, None] * inv[None, :]
    ang_c = col[:, None] * inv[None, :]
    ang = jnp.concatenate([ang_r, ang_r, ang_c, ang_c], axis=-1)
    sign = jnp.tile(jnp.concatenate([-jnp.ones((half // 2,), F32), jnp.ones((half // 2,), F32)]), 2)
    cos = jnp.cos(ang)
    sin = jnp.sin(ang) * sign[None, :]
    pad = LANE - rot_dim
    cos = jnp.pad(cos, ((0, ident_rows), (0, pad)), constant_values=1.0)
    sin = jnp.pad(sin, ((0, ident_rows), (0, pad)), constant_values=0.0)
    return cos, sin


def _norm_matmul_kernel(*refs, modulate, rope_half):
    it = iter(refs)
    x_ref, nw_ref = next(it), next(it)
    sc_ref = sh_ref = cos_ref = sin_ref = None
    if modulate:
        sc_ref, sh_ref = next(it), next(it)
    w_ref = next(it)
    if rope_half:
        cos_ref, sin_ref = next(it), next(it)
    o_ref, h_ref = next(it), next(it)

    @pl.when(pl.program_id(1) == 0)
    def _():
        y = _rmsnorm(x_ref[...], nw_ref[...])
        if modulate:
            y = y * (1.0 + sc_ref[...]) + sh_ref[...]
        h_ref[...] = y.astype(BF16)

    acc = jnp.dot(h_ref[...], w_ref[...], preferred_element_type=F32)
    if rope_half:
        cos, sin = cos_ref[...], sin_ref[...]
        for c in range(acc.shape[1] // LANE):
            sl = slice(c * LANE, (c + 1) * LANE)
            o_ref[:, sl] = _rope_lanes(acc[:, sl], cos, sin, rope_half).astype(o_ref.dtype)
    else:
        o_ref[...] = acc.astype(o_ref.dtype)


def _norm_matmul(x, x_col, nw, w, rows_p, seq_s, *, mod=None, layer=0, m_shift=0, m_scale=0,
                 rope=None, out_dtype=F32, name):
    t = x.shape[0]
    k, n = w.shape
    tm = _tile(math.gcd(rows_p, seq_s), ROW_TILE)
    tn = _tile(n, COL_TILE, LANE)
    p_tiles = rows_p // tm
    s_tiles = seq_s // tm

    def grp(i):
        return _group_of_row(i * tm, rows_p, seq_s)

    in_specs = [pl.BlockSpec((tm, k), lambda i, j: (i, x_col)),
                pl.BlockSpec((1, k), lambda i, j: (0, 0))]
    args = [x, nw.reshape(1, k)]
    if mod is not None:
        in_specs += [pl.BlockSpec((None, None, None, 1, k), lambda i, j: (layer, m_scale, grp(i), 0, 0)),
                     pl.BlockSpec((None, None, None, 1, k), lambda i, j: (layer, m_shift, grp(i), 0, 0))]
        args += [mod, mod]
    in_specs.append(pl.BlockSpec((k, tn), lambda i, j: (0, j)))
    args.append(w)
    rope_half = 0
    if rope is not None:
        cos, sin, rope_half, first_tile = rope

        def pos_tile(i, j):
            is_pos = jnp.logical_and(i >= p_tiles, j >= first_tile)
            return jnp.where(is_pos, jnp.maximum(i - p_tiles, 0) % s_tiles, s_tiles)

        in_specs += [pl.BlockSpec((tm, LANE), lambda i, j: (pos_tile(i, j), 0)),
                     pl.BlockSpec((tm, LANE), lambda i, j: (pos_tile(i, j), 0))]
        args += [cos, sin]
    osz = jnp.dtype(out_dtype).itemsize
    vmem = 2 * (tm * k * 4 + k * tn * 2 + tm * tn * osz) + tm * k * 2 + 3 * tm * max(k, tn) * 4 + 4 * MIB
    return pl.pallas_call(
        functools.partial(_norm_matmul_kernel, modulate=mod is not None, rope_half=rope_half),
        grid=(t // tm, n // tn),
        in_specs=in_specs,
        out_specs=pl.BlockSpec((tm, tn), lambda i, j: (i, j)),
        out_shape=jax.ShapeDtypeStruct((t, n), out_dtype),
        scratch_shapes=[pltpu.VMEM((tm, k), BF16)],
        compiler_params=_params(("parallel", "arbitrary"), vmem),
        name=name,
    )(*args)


def _matmul_kernel(*refs, scale, aliased):
    a_ref, b_ref = refs[0], refs[1]
    o_ref, acc_ref = refs[-2], refs[-1]
    del aliased
    kk = pl.program_id(3)

    @pl.when(kk == 0)
    def _():
        acc_ref[...] = jnp.zeros_like(acc_ref)

    acc_ref[...] += jnp.dot(a_ref[...], b_ref[...], preferred_element_type=F32)

    @pl.when(kk == pl.num_programs(3) - 1)
    def _():
        acc = acc_ref[...]
        if scale != 1.0:
            acc = acc * scale
        o_ref[...] = acc.astype(o_ref.dtype)


def _matmul(a, b, *, grid, a_spec, b_spec, o_spec, out_shape, tile, scale=1.0, prev=None, name):
    tm, tn, tk = tile
    in_specs = [a_spec, b_spec]
    args = [a, b]
    aliases = {}
    if prev is not None:
        in_specs.append(pl.BlockSpec(memory_space=pl.ANY))
        args.append(prev)
        aliases = {2: 0}
    osz = jnp.dtype(out_shape.dtype).itemsize
    vmem = 2 * (tm * tk * 2 + tk * tn * 2 + tm * tn * osz) + 2 * tm * tn * 4 + 4 * MIB
    return pl.pallas_call(
        functools.partial(_matmul_kernel, scale=scale, aliased=prev is not None),
        grid=grid,
        in_specs=in_specs,
        out_specs=o_spec,
        out_shape=out_shape,
        scratch_shapes=[pltpu.VMEM((tm, tn), F32)],
        input_output_aliases=aliases,
        compiler_params=_params(("parallel", "parallel", "parallel", "arbitrary"), vmem),
        name=name,
    )(*args)


def _out_proj_kernel(a1_ref, a2_ref, w1_ref, w2_ref, x_ref, g_ref, o_ref):
    acc = jnp.dot(a1_ref[...], w1_ref[...], preferred_element_type=F32)
    acc += jnp.dot(a2_ref[...], w2_ref[...], preferred_element_type=F32)
    o_ref[...] = x_ref[...] + g_ref[...] * acc


def _out_proj(a1, a2, w1, w2, x, mod, layer, m_gate, rows_p, seq_s, *, name):
    t, d = x.shape
    k1, k2 = a1.shape[1], a2.shape[1]
    tm = _tile(math.gcd(rows_p, seq_s), ROW_TILE)
    tn = _tile(d, 2 * COL_TILE, LANE)

    def grp(i):
        return _group_of_row(i * tm, rows_p, seq_s)

    vmem = 2 * (tm * (k1 + k2) * 2 + (k1 + k2) * tn * 2 + 2 * tm * tn * 4) + 2 * tm * tn * 4 + 4 * MIB
    return pl.pallas_call(
        _out_proj_kernel,
        grid=(t // tm, d // tn),
        in_specs=[
            pl.BlockSpec((tm, k1), lambda i, j: (i, 0)),
            pl.BlockSpec((tm, k2), lambda i, j: (i, 0)),
            pl.BlockSpec((k1, tn), lambda i, j: (0, j)),
            pl.BlockSpec((k2, tn), lambda i, j: (0, j)),
            pl.BlockSpec((tm, tn), lambda i, j: (i, j)),
            pl.BlockSpec((None, None, None, 1, tn), lambda i, j: (layer, m_gate, grp(i), 0, j)),
        ],
        out_specs=pl.BlockSpec((tm, tn), lambda i, j: (i, j)),
        out_shape=jax.ShapeDtypeStruct((t, d), F32),
        compiler_params=_params(("parallel", "arbitrary"), vmem),
        name=name,
    )(a1, a2, w1, w2, x, mod)


def _qk_post_kernel(q_ref, k_ref, v_ref, qw_ref, kw_ref, cos_ref, sin_ref,
                    qo_ref, ko_ref, vo_ref, ks_ref):
    cos, sin = cos_ref[...], sin_ref[...]
    qw, kw = qw_ref[...], kw_ref[...]
    for h in range(q_ref.shape[1] // HEAD_DIM):
        sl = slice(h * HEAD_DIM, (h + 1) * HEAD_DIM)
        y = _rope_lanes(_rmsnorm(q_ref[:, sl], qw), cos, sin, HEAD_DIM // 4)
        qo_ref[:, sl] = y.astype(BF16)
    for h in range(k_ref.shape[1] // HEAD_DIM):
        sl = slice(h * HEAD_DIM, (h + 1) * HEAD_DIM)
        y = _rope_lanes(_rmsnorm(k_ref[:, sl], kw), cos, sin, HEAD_DIM // 4)
        ks_ref[:, sl] = y
        ko_ref[:, sl] = y.astype(BF16)
    vo_ref[...] = v_ref[...].astype(BF16)


def _qk_post(proj, q_norm, k_norm, cos, sin, rows_p, seq_s):
    t = proj.shape[0]
    qw_, kvw = A_HEADS * HEAD_DIM, A_KV_HEADS * HEAD_DIM
    tm = _tile(math.gcd(rows_p, seq_s), POST_ROW_TILE)
    p_tiles, s_tiles = rows_p // tm, seq_s // tm
    kblk = qw_ // kvw

    def pos_tile(i):
        return jnp.where(i >= p_tiles, jnp.maximum(i - p_tiles, 0) % s_tiles, s_tiles)

    vmem = 2 * (tm * (qw_ + 2 * kvw) * 4 + tm * (qw_ + 2 * kvw) * 2 + tm * kvw * 4) + 4 * tm * qw_ * 4 + 4 * MIB
    return pl.pallas_call(
        _qk_post_kernel,
        grid=(t // tm,),
        in_specs=[
            pl.BlockSpec((tm, qw_), lambda i: (i, 0)),
            pl.BlockSpec((tm, kvw), lambda i: (i, kblk)),
            pl.BlockSpec((tm, kvw), lambda i: (i, kblk + 1)),
            pl.BlockSpec((1, HEAD_DIM), lambda i: (0, 0)),
            pl.BlockSpec((1, HEAD_DIM), lambda i: (0, 0)),
            pl.BlockSpec((tm, LANE), lambda i: (pos_tile(i), 0)),
            pl.BlockSpec((tm, LANE), lambda i: (pos_tile(i), 0)),
        ],
        out_specs=[
            pl.BlockSpec((tm, qw_), lambda i: (i, 0)),
            pl.BlockSpec((tm, kvw), lambda i: (i, 0)),
            pl.BlockSpec((tm, kvw), lambda i: (i, 0)),
            pl.BlockSpec((tm, kvw), lambda i: (i, 0)),
        ],
        out_shape=[
            jax.ShapeDtypeStruct((t, qw_), BF16),
            jax.ShapeDtypeStruct((t, kvw), BF16),
            jax.ShapeDtypeStruct((t, kvw), BF16),
            jax.ShapeDtypeStruct((t, kvw), F32),
        ],
        compiler_params=_params(("parallel",), vmem),
        name="even_qk_norm_rope",
    )(proj, proj, proj, q_norm.reshape(1, HEAD_DIM), k_norm.reshape(1, HEAD_DIM), cos, sin)


def _attention_kernel(*refs, group, scale, extra, aliased):
    it = iter(refs)
    q_ref, k_ref, v_ref = next(it), next(it), next(it)
    q2_ref = k2_ref = None
    if extra:
        q2_ref, k2_ref = next(it), next(it)
    if aliased:
        next(it)
    o_ref = next(it)
    k = k_ref[...]
    v = v_ref[...]
    if extra:
        k = jnp.concatenate([k, k2_ref[...]], axis=-1)
    for h in range(group):
        sl = slice(h * LANE, (h + 1) * LANE)
        q = q_ref[:, sl]
        if extra:
            q = jnp.concatenate([q, q2_ref[:, sl]], axis=-1)
        s = lax.dot_general(q, k, (((1,), (1,)), ((), ())), preferred_element_type=F32) * scale
        m = jnp.max(s, axis=-1, keepdims=True)
        p = jnp.exp(s - m)
        den = jnp.sum(p, axis=-1, keepdims=True)
        o = jnp.dot(p.astype(BF16), v, preferred_element_type=F32) / den
        o_ref[:, sl] = o.astype(o_ref.dtype)


def _attention(q, k, v, *, n_batch, sq, nk, q_row0, k_blk0, k_col0, v_col0, n_kv, group, scale,
               out_cols, q2=None, q2_col0=0, k2=None, k2_blk0=0, prev=None, name):
    t = q.shape[0]
    tq = _tile(math.gcd(sq, q_row0), ATTN_Q_TILE)
    qb0 = q_row0 // tq
    qpb = sq // tq
    gw = group * LANE
    extra = q2 is not None
    in_specs = [
        pl.BlockSpec((tq, gw), lambda b, g, i: (qb0 + b * qpb + i, g)),
        pl.BlockSpec((nk, LANE), lambda b, g, i: (k_blk0 + b, k_col0 + g)),
        pl.BlockSpec((nk, LANE), lambda b, g, i: (k_blk0 + b, v_col0 + g)),
    ]
    args = [q, k, v]
    if extra:
        in_specs += [pl.BlockSpec((tq, gw), lambda b, g, i: (qb0 + b * qpb + i, q2_col0 + g)),
                     pl.BlockSpec((nk, LANE), lambda b, g, i: (k2_blk0 + b, 0))]
        args += [q, k2]
    aliases = {}
    if prev is not None:
        in_specs.append(pl.BlockSpec(memory_space=pl.ANY))
        aliases = {len(args): 0}
        args.append(prev)
    kd = 2 * LANE if extra else LANE
    vmem = (2 * (tq * gw * 2 * (2 if extra else 1) + nk * LANE * 2 * (3 if extra else 2) + tq * gw * 2)
            + nk * kd * 2 + 3 * tq * nk * 4 + 4 * MIB)
    return pl.pallas_call(
        functools.partial(_attention_kernel, group=group, scale=scale, extra=extra, aliased=prev is not None),
        grid=(n_batch, n_kv, qpb),
        in_specs=in_specs,
        out_specs=pl.BlockSpec((tq, gw), lambda b, g, i: (qb0 + b * qpb + i, g)),
        out_shape=jax.ShapeDtypeStruct((t, out_cols), BF16),
        input_output_aliases=aliases,
        compiler_params=_params(("parallel", "parallel", "parallel"), vmem),
        name=name,
    )(*args)


def _dft_width_kernel(f_ref, cs_ref, a_ref, b_ref):
    r = jnp.dot(f_ref[...].astype(BF16), cs_ref[...], preferred_element_type=F32)
    a_ref[...] = r[:, :B_GROUP_W].astype(BF16)
    b_ref[...] = r[:, B_GROUP_W:].astype(BF16)


def _dft_tables(n):
    j = jnp.arange(n, dtype=I32)
    jk = (j[:, None] * j[None, :]) % n
    ang = jk.astype(F32) * (2.0 * math.pi / n)
    return jnp.cos(ang), jnp.sin(ang)


def _dft_width(proj, col0_blk, rows_p, seq_s):
    t = proj.shape[0]
    bw = B_GROUPS * B_GROUP_W
    cw, sw = _dft_tables(B_GROUP_W)
    cs = jnp.concatenate([cw, -sw], axis=1).astype(BF16)
    tm = _tile(math.gcd(rows_p, seq_s), 2 * ROW_TILE)
    vmem = 2 * (tm * LANE * 4 + 2 * tm * LANE * 2) + 4 * tm * LANE * 4 + 4 * MIB
    a, b = pl.pallas_call(
        _dft_width_kernel,
        grid=(t // tm, B_GROUPS),
        in_specs=[pl.BlockSpec((tm, B_GROUP_W), lambda i, g: (i, col0_blk + g)),
                  pl.BlockSpec((B_GROUP_W, 2 * B_GROUP_W), lambda i, g: (0, 0))],
        out_specs=[pl.BlockSpec((tm, B_GROUP_W), lambda i, g: (i, g)),
                   pl.BlockSpec((tm, B_GROUP_W), lambda i, g: (i, g))],
        out_shape=[jax.ShapeDtypeStruct((t, bw), BF16), jax.ShapeDtypeStruct((t, bw), BF16)],
        compiler_params=_params(("parallel", "parallel"), vmem),
        name="fourier_width_dft",
    )(proj, cs)
    return jnp.stack([a, b], axis=0)


def _dft_seq(ab, n_batch, seq, row0, prev, name):
    _, t, bw = ab.shape
    c, s = _dft_tables(seq)
    f = jnp.concatenate([c, s], axis=1).astype(BF16)
    tm = _tile(math.gcd(seq, row0), 2 * ROW_TILE)
    tk = tm
    tn = _tile(bw, 2 * COL_TILE, LANE)
    kt = seq // tk
    rb0 = row0 // tk
    ob0 = row0 // tm
    scale = 1.0 / math.sqrt(seq * B_GROUP_W)
    return _matmul(
        f, ab,
        grid=(n_batch, seq // tm, bw // tn, 2 * kt),
        a_spec=pl.BlockSpec((tm, tk), lambda b, i, j, kk: (i, kk)),
        b_spec=pl.BlockSpec((None, tk, tn), lambda b, i, j, kk: (kk // kt, rb0 + b * kt + kk % kt, j)),
        o_spec=pl.BlockSpec((tm, tn), lambda b, i, j, kk: (ob0 + b * (seq // tm) + i, j)),
        out_shape=jax.ShapeDtypeStruct((t, bw), BF16),
        tile=(tm, tn, tk), scale=scale, prev=prev, name=name)


def _kv_post_kernel(ckv_ref, kr_ref, w_ref, cos_ref, sin_ref, cn_ref, cb_ref, kr_out_ref):
    y = _rmsnorm(ckv_ref[...], w_ref[...])
    cn_ref[...] = y
    cb_ref[...] = y.astype(BF16)
    kr_out_ref[...] = _rope_lanes(kr_ref[...], cos_ref[...], sin_ref[...], QK_ROPE // 4).astype(BF16)


def _kv_post(proj, ckv_blk, kr_blk, kv_norm, cos, sin, rows_p, seq_s):
    t = proj.shape[0]
    tm = _tile(math.gcd(rows_p, seq_s), POST_ROW_TILE)
    p_tiles, s_tiles = rows_p // tm, seq_s // tm

    def pos_tile(i):
        return jnp.where(i >= p_tiles, jnp.maximum(i - p_tiles, 0) % s_tiles, s_tiles)

    vmem = 2 * (tm * KV_LORA * 10 + tm * LANE * 14) + 4 * tm * KV_LORA * 4 + 4 * MIB
    return pl.pallas_call(
        _kv_post_kernel,
        grid=(t // tm,),
        in_specs=[
            pl.BlockSpec((tm, KV_LORA), lambda i: (i, ckv_blk)),
            pl.BlockSpec((tm, LANE), lambda i: (i, kr_blk)),
            pl.BlockSpec((1, KV_LORA), lambda i: (0, 0)),
            pl.BlockSpec((tm, LANE), lambda i: (pos_tile(i), 0)),
            pl.BlockSpec((tm, LANE), lambda i: (pos_tile(i), 0)),
        ],
        out_specs=[
            pl.BlockSpec((tm, KV_LORA), lambda i: (i, 0)),
            pl.BlockSpec((tm, KV_LORA), lambda i: (i, 0)),
            pl.BlockSpec((tm, LANE), lambda i: (i, 0)),
        ],
        out_shape=[
            jax.ShapeDtypeStruct((t, KV_LORA), F32),
            jax.ShapeDtypeStruct((t, KV_LORA), BF16),
            jax.ShapeDtypeStruct((t, LANE), BF16),
        ],
        compiler_params=_params(("parallel",), vmem),
        name="odd_kv_norm_rope",
    )(proj, proj, kv_norm.reshape(1, KV_LORA), cos, sin)


def _conv_kernel(u_ref, gb_ref, gc_ref, up_ref, gcp_ref, un_ref, gcn_ref, w_ref, o_ref, *,
                 tr, rows_p, seq_p, seq_s):
    i = pl.program_id(0)
    start = i * tr
    local = jnp.where(start < rows_p, start % seq_p, jnp.maximum(start - rows_p, 0) % seq_s)
    seq = jnp.where(start < rows_p, seq_p, seq_s)
    has_prev = local > 0
    has_next = local + tr < seq
    z = gc_ref[...] * u_ref[...]
    zp = (gcp_ref[...] * up_ref[...])[SUBLANE - 1:SUBLANE, :]
    zn = (gcn_ref[...] * un_ref[...])[0:1, :]
    zp = jnp.where(has_prev, zp, 0.0)
    zn = jnp.where(has_next, zn, 0.0)
    row = lax.broadcasted_iota(I32, z.shape, 0)
    z_prev = jnp.where(row == 0, zp, pltpu.roll(z, 1, 0))
    z_next = jnp.where(row == tr - 1, zn, pltpu.roll(z, tr - 1, 0))
    w = w_ref[...]
    conv = z_prev * w[0:1, :] + z * w[1:2, :] + z_next * w[2:3, :]
    o_ref[...] = (gb_ref[...] * conv).astype(o_ref.dtype)


def _gated_conv(proj, u_blk, gb_blk, gc_blk, conv_w, rows_p, seq_p, seq_s):
    t = proj.shape[0]
    tr = _tile(math.gcd(seq_p, seq_s), POST_ROW_TILE)
    tc = _tile(D_CH, COL_TILE, LANE)
    halo = tr // SUBLANE
    last = t // SUBLANE - 1
    cpb = D_CH // tc

    def main(blk):
        return pl.BlockSpec((tr, tc), lambda i, j: (i, blk * cpb + j))

    def prev(blk):
        return pl.BlockSpec((SUBLANE, tc), lambda i, j: (jnp.maximum(i * halo - 1, 0), blk * cpb + j))

    def nxt(blk):
        return pl.BlockSpec((SUBLANE, tc), lambda i, j: (jnp.minimum((i + 1) * halo, last), blk * cpb + j))

    w8 = jnp.pad(conv_w, ((0, SUBLANE - CONV_W), (0, 0)))
    vmem = 2 * (3 * tr * tc * 4 + 4 * SUBLANE * tc * 4 + tr * tc * 2) + 8 * tr * tc * 4 + 4 * MIB
    return pl.pallas_call(
        functools.partial(_conv_kernel, tr=tr, rows_p=rows_p, seq_p=seq_p, seq_s=seq_s),
        grid=(t // tr, cpb),
        in_specs=[main(u_blk), main(gb_blk), main(gc_blk), prev(u_blk), prev(gc_blk), nxt(u_blk), nxt(gc_blk),
                  pl.BlockSpec((SUBLANE, tc), lambda i, j: (0, j))],
        out_specs=pl.BlockSpec((tr, tc), lambda i, j: (i, j)),
        out_shape=jax.ShapeDtypeStruct((t, D_CH), BF16),
        compiler_params=_params(("parallel", "parallel"), vmem),
        name="odd_gated_conv",
    )(proj, proj, proj, proj, proj, proj, proj, w8)


def _router_kernel(x_ref, nw_ref, sc_ref, sh_ref, rw_ref, rb_ref,
                   h_ref, idx_ref, wt_ref, rank_ref, cnt_ref, carry_ref):
    @pl.when(pl.program_id(0) == 0)
    def _():
        carry_ref[...] = jnp.zeros_like(carry_ref)

    h = _rmsnorm(x_ref[...], nw_ref[...]) * (1.0 + sc_ref[...]) + sh_ref[...]
    h_ref[...] = h
    logits = lax.dot_general(rw_ref[...], h, (((1,), (1,)), ((), ())),
                             precision=lax.Precision.HIGHEST, preferred_element_type=F32) + rb_ref[...]
    n_e, tm = logits.shape
    eidx = lax.broadcasted_iota(I32, logits.shape, 0).astype(F32)
    work = logits
    vals, hots = [], []
    for k in range(TOP_K):
        m = jnp.max(work, axis=0, keepdims=True)
        sel = jnp.min(jnp.where(work == m, eidx, float(n_e)), axis=0, keepdims=True)
        hot = eidx == sel
        idx_ref[k:k + 1, :] = sel.astype(I32)
        vals.append(m)
        hots.append(hot)
        work = jnp.where(hot, -jnp.inf, work)
    exps = [jnp.exp(v - vals[0]) for v in vals]
    den = exps[0]
    for e in exps[1:]:
        den = den + e
    for k in range(TOP_K):
        wt_ref[k:k + 1, :] = exps[k] / den
    chosen = hots[0]
    for hot in hots[1:]:
        chosen = jnp.logical_or(chosen, hot)
    chosen = jnp.where(chosen, 1.0, 0.0)
    r = lax.broadcasted_iota(I32, (tm, tm), 0)
    c = lax.broadcasted_iota(I32, (tm, tm), 1)
    upper = jnp.where(r < c, 1.0, 0.0).astype(BF16)
    before = jnp.dot(chosen.astype(BF16), upper, preferred_element_type=F32) + carry_ref[:, 0:1]
    for k in range(TOP_K):
        rk = jnp.sum(jnp.where(hots[k], before, 0.0), axis=0, keepdims=True)
        rank_ref[k:k + 1, :] = rk.astype(I32)
    carry_ref[...] = carry_ref[...] + jnp.sum(chosen, axis=1, keepdims=True)
    cnt_ref[...] = carry_ref[...].astype(I32)


def _router(x, nw, mod, layer, router_w, router_b, rows_p, seq_s):
    t, d = x.shape
    n_e = router_w.shape[1]
    tm = _tile(math.gcd(rows_p, seq_s), ROW_TILE, LANE)

    def grp(i):
        return _group_of_row(i * tm, rows_p, seq_s)

    vmem = 2 * (2 * tm * d * 4 + n_e * d * 4) + 4 * tm * d * 4 + tm * tm * 4 + 4 * MIB
    return pl.pallas_call(
        _router_kernel,
        grid=(t // tm,),
        in_specs=[
            pl.BlockSpec((tm, d), lambda i: (i, 0)),
            pl.BlockSpec((1, d), lambda i: (0, 0)),
            pl.BlockSpec((None, None, None, 1, d), lambda i: (layer, 4, grp(i), 0, 0)),
            pl.BlockSpec((None, None, None, 1, d), lambda i: (layer, 3, grp(i), 0, 0)),
            pl.BlockSpec((n_e, d), lambda i: (0, 0)),
            pl.BlockSpec((n_e, 1), lambda i: (0, 0)),
        ],
        out_specs=[
            pl.BlockSpec((tm, d), lambda i: (i, 0)),
            pl.BlockSpec((TOP_K, tm), lambda i: (0, i)),
            pl.BlockSpec((TOP_K, tm), lambda i: (0, i)),
            pl.BlockSpec((TOP_K, tm), lambda i: (0, i)),
            pl.BlockSpec((n_e, LANE), lambda i: (0, 0)),
        ],
        out_shape=[
            jax.ShapeDtypeStruct((t, d), F32),
            jax.ShapeDtypeStruct((TOP_K, t), I32),
            jax.ShapeDtypeStruct((TOP_K, t), F32),
            jax.ShapeDtypeStruct((TOP_K, t), I32),
            jax.ShapeDtypeStruct((n_e, LANE), I32),
        ],
        scratch_shapes=[pltpu.VMEM((n_e, LANE), F32)],
        compiler_params=_params(("arbitrary",), vmem),
        name="moe_router",
    )(x, nw.reshape(1, d), mod, mod, router_w.T, router_b.reshape(n_e, 1))


def _row_copy(src_hbm, row, dst, plane, r, sem, slot):
    return pltpu.make_async_copy(src_hbm.at[pl.ds(row, 1), :], dst.at[plane, pl.ds(r, 1), :], sem.at[slot])


def _gather_kernel(nv_ref, tok_ref, tok_next_ref, h_hbm, o_ref, buf, sem):
    i = pl.program_id(0)
    nv = nv_ref[0]
    rows = buf.shape[1]

    def start_tile(tok, slot):
        def body(r, carry):
            _row_copy(h_hbm, tok[0, r], buf, slot, r, sem, slot).start()
            return carry
        lax.fori_loop(0, rows, body, 0, unroll=8)

    @pl.when(i == 0)
    def _():
        start_tile(tok_ref, 0)

    @pl.when(i + 1 < nv)
    def _():
        start_tile(tok_next_ref, (i + 1) % 2)

    @pl.when(i < nv)
    def _():
        slot = i % 2

        def body(r, carry):
            _row_copy(h_hbm, 0, buf, slot, r, sem, slot).wait()
            return carry
        lax.fori_loop(0, rows, body, 0, unroll=8)
        o_ref[...] = buf[slot].astype(o_ref.dtype)

    @pl.when(i >= nv)
    def _():
        o_ref[...] = jnp.zeros_like(o_ref)


def _gather_rows(h, token_of, n_valid, tg):
    nt = token_of.shape[0]
    d = h.shape[1]
    vmem = 2 * tg * d * 4 + 2 * tg * d * 2 + tg * d * 4 + 4 * MIB
    grid_spec = pltpu.PrefetchScalarGridSpec(
        num_scalar_prefetch=1,
        grid=(nt,),
        in_specs=[
            pl.BlockSpec((None, 1, tg), lambda i, nv: (jnp.minimum(i, nv[0] - 1), 0, 0), memory_space=pltpu.SMEM),
            pl.BlockSpec((None, 1, tg), lambda i, nv: (jnp.minimum(i + 1, nv[0] - 1), 0, 0), memory_space=pltpu.SMEM),
            pl.BlockSpec(memory_space=pl.ANY),
        ],
        out_specs=pl.BlockSpec((tg, d), lambda i, nv: (i, 0)),
        scratch_shapes=[pltpu.VMEM((2, tg, d), F32), pltpu.SemaphoreType.DMA((2,))],
    )
    return pl.pallas_call(
        _gather_kernel,
        grid_spec=grid_spec,
        out_shape=jax.ShapeDtypeStruct((nt * tg, d), BF16),
        compiler_params=_params(("arbitrary",), vmem),
        name="moe_dispatch_gather",
    )(n_valid, token_of, token_of, h)


def _gate_up_kernel(te_ref, nv_ref, x_ref, wg_ref, wu_ref, bg_ref, bu_ref, o_ref, wg_bf, wu_bf):
    r = pl.program_id(1)
    e = te_ref[r]
    e_prev = te_ref[jnp.maximum(r - 1, 0)]

    @pl.when(jnp.logical_or(r == 0, e != e_prev))
    def _():
        wg_bf[...] = wg_ref[...].astype(BF16)
        wu_bf[...] = wu_ref[...].astype(BF16)

    @pl.when(r < nv_ref[0])
    def _():
        x = x_ref[...]
        g = jnp.dot(x, wg_bf[...], preferred_element_type=F32) + bg_ref[...]
        u = jnp.dot(x, wu_bf[...], preferred_element_type=F32) + bu_ref[...]
        g = jnp.minimum(g, SWIGLU_LIMIT)
        u = jnp.clip(u, -SWIGLU_LIMIT, SWIGLU_LIMIT)
        o_ref[...] = ((u + 1.0) * g * jax.nn.sigmoid(SWIGLU_ALPHA * g)).astype(o_ref.dtype)

    @pl.when(r >= nv_ref[0])
    def _():
        o_ref[...] = jnp.zeros_like(o_ref)


def _gate_up(xs, tile_expert, n_valid, layer, w_gate, w_up, b_gate, b_up, tm):
    p, d = xs.shape
    depth, n_e, _, ff = w_gate.shape
    nt = p // tm
    nc = _tile(ff, GATE_UP_COL_TILE, LANE)

    def row(c, r, te, nv):
        return jnp.minimum(r, nv[0] - 1)

    vmem = 2 * (tm * d * 2 + 2 * d * nc * 4 + tm * nc * 2) + 2 * d * nc * 2 + 6 * tm * nc * 4 + 4 * MIB
    grid_spec = pltpu.PrefetchScalarGridSpec(
        num_scalar_prefetch=2,
        grid=(ff // nc, nt),
        in_specs=[
            pl.BlockSpec((tm, d), lambda c, r, te, nv: (row(c, r, te, nv), 0)),
            pl.BlockSpec((None, None, d, nc), lambda c, r, te, nv: (layer, te[r], 0, c)),
            pl.BlockSpec((None, None, d, nc), lambda c, r, te, nv: (layer, te[r], 0, c)),
            pl.BlockSpec((None, None, 1, nc), lambda c, r, te, nv: (layer, te[r], 0, c)),
            pl.BlockSpec((None, None, 1, nc), lambda c, r, te, nv: (layer, te[r], 0, c)),
        ],
        out_specs=pl.BlockSpec((tm, nc), lambda c, r, te, nv: (r, c)),
        scratch_shapes=[pltpu.VMEM((d, nc), BF16), pltpu.VMEM((d, nc), BF16)],
    )
    return pl.pallas_call(
        _gate_up_kernel,
        grid_spec=grid_spec,
        out_shape=jax.ShapeDtypeStruct((p, ff), BF16),
        compiler_params=_params(("arbitrary", "arbitrary"), vmem),
        name="moe_gate_up",
    )(tile_expert, n_valid, xs, w_gate, w_up, b_gate.reshape(depth, n_e, 1, ff), b_up.reshape(depth, n_e, 1, ff))


def _down_kernel(te_ref, nv_ref, a_ref, wd_ref, bd_ref, o_ref, wd_bf):
    r = pl.program_id(1)
    e = te_ref[r]
    e_prev = te_ref[jnp.maximum(r - 1, 0)]

    @pl.when(jnp.logical_or(r == 0, e != e_prev))
    def _():
        wd_bf[...] = wd_ref[...].astype(BF16)

    @pl.when(r < nv_ref[0])
    def _():
        o_ref[...] = jnp.dot(a_ref[...], wd_bf[...], preferred_element_type=F32) + bd_ref[...]

    @pl.when(r >= nv_ref[0])
    def _():
        o_ref[...] = jnp.zeros_like(o_ref)


def _down(act, tile_expert, n_valid, layer, w_down, b_down, tm):
    p, ff = act.shape
    depth, n_e, _, d = w_down.shape
    nt = p // tm
    nc = _tile(d, DOWN_COL_TILE, LANE)

    def row(c, r, te, nv):
        return jnp.minimum(r, nv[0] - 1)

    vmem = 2 * (tm * ff * 2 + ff * nc * 4 + tm * nc * 4) + ff * nc * 2 + 2 * tm * nc * 4 + 4 * MIB
    grid_spec = pltpu.PrefetchScalarGridSpec(
        num_scalar_prefetch=2,
        grid=(d // nc, nt),
        in_specs=[
            pl.BlockSpec((tm, ff), lambda c, r, te, nv: (row(c, r, te, nv), 0)),
            pl.BlockSpec((None, None, ff, nc), lambda c, r, te, nv: (layer, te[r], 0, c)),
            pl.BlockSpec((None, None, 1, nc), lambda c, r, te, nv: (layer, te[r], 0, c)),
        ],
        out_specs=pl.BlockSpec((tm, nc), lambda c, r, te, nv: (r, c)),
        scratch_shapes=[pltpu.VMEM((ff, nc), BF16)],
    )
    return pl.pallas_call(
        _down_kernel,
        grid_spec=grid_spec,
        out_shape=jax.ShapeDtypeStruct((p, d), F32),
        compiler_params=_params(("arbitrary", "arbitrary"), vmem),
        name="moe_down",
    )(tile_expert, n_valid, act, w_down, b_down.reshape(depth, n_e, 1, d))


def _combine_kernel(pos_ref, pos_next_ref, x_ref, wt_ref, g_ref, fw_ref, y_hbm, o_ref, buf, sem, *, final_norm):
    i = pl.program_id(0)
    n = pl.num_programs(0)
    rows = x_ref.shape[0]

    def start_tile(pos, slot):
        def body(r, carry):
            for k in range(TOP_K):
                _row_copy(y_hbm, pos[k, r], buf, slot * TOP_K + k, r, sem, slot).start()
            return carry
        lax.fori_loop(0, rows, body, 0, unroll=4)

    @pl.when(i == 0)
    def _():
        start_tile(pos_ref, 0)

    @pl.when(i + 1 < n)
    def _():
        start_tile(pos_next_ref, (i + 1) % 2)

    slot = i % 2

    def wait_body(r, carry):
        for k in range(TOP_K):
            _row_copy(y_hbm, 0, buf, slot * TOP_K + k, r, sem, slot).wait()
        return carry
    lax.fori_loop(0, rows, wait_body, 0, unroll=4)

    wt = wt_ref[...]
    acc = wt[:, 0:1] * buf[slot * TOP_K]
    for k in range(1, TOP_K):
        acc = acc + wt[:, k:k + 1] * buf[slot * TOP_K + k]
    y = x_ref[...] + g_ref[...] * acc
    if final_norm:
        y = _rmsnorm(y, fw_ref[...])
    o_ref[...] = y


def _combine(x, y_pairs, pos, wts, mod, layer, final_w, rows_p, seq_s, *, final_norm):
    t, d = x.shape
    tc = _tile(math.gcd(rows_p, seq_s), COMBINE_ROW_TILE)
    nt = t // tc
    pos3 = pos.reshape(TOP_K, nt, tc).transpose(1, 0, 2)

    def grp(i):
        return _group_of_row(i * tc, rows_p, seq_s)

    vmem = 2 * TOP_K * tc * d * 4 + 2 * (2 * tc * d * 4 + tc * LANE * 4) + 3 * tc * d * 4 + 4 * MIB
    return pl.pallas_call(
        functools.partial(_combine_kernel, final_norm=final_norm),
        grid=(nt,),
        in_specs=[
            pl.BlockSpec((None, TOP_K, tc), lambda i: (i, 0, 0), memory_space=pltpu.SMEM),
            pl.BlockSpec((None, TOP_K, tc), lambda i: (jnp.minimum(i + 1, nt - 1), 0, 0), memory_space=pltpu.SMEM),
            pl.BlockSpec((tc, d), lambda i: (i, 0)),
            pl.BlockSpec((tc, TOP_K), lambda i: (i, 0)),
            pl.BlockSpec((None, None, None, 1, d), lambda i: (layer, 5, grp(i), 0, 0)),
            pl.BlockSpec((1, d), lambda i: (0, 0)),
            pl.BlockSpec(memory_space=pl.ANY),
        ],
        out_specs=pl.BlockSpec((tc, d), lambda i: (i, 0)),
        out_shape=jax.ShapeDtypeStruct((t, d), F32),
        scratch_shapes=[pltpu.VMEM((2 * TOP_K, tc, d), F32), pltpu.SemaphoreType.DMA((2,))],
        compiler_params=_params(("arbitrary",), vmem),
        name="moe_combine",
    )(pos3, pos3, x, wts.T, mod, final_w.reshape(1, d), y_pairs)


def _moe_layer(x, nw, mod, layer, router_w, router_b, w_gate, b_gate, w_up, b_up, w_down, b_down,
               final_w, rows_p, seq_s, *, final_norm):
    t, d = x.shape
    n_e = router_w.shape[1]
    tm = EXPERT_ROW_TILE
    tg = _tile(tm, GATHER_ROW_TILE)
    h, idx, wts, rank, cnt = _router(x, nw, mod, layer, router_w, router_b, rows_p, seq_s)
    counts = cnt[:, 0]
    tiles_e = (counts + tm - 1) // tm
    tile_end = jnp.cumsum(tiles_e)
    offs = (tile_end - tiles_e) * tm
    pos = offs[idx] + rank
    n_tiles = (t * TOP_K + tm - 1) // tm + n_e
    p_rows = n_tiles * tm
    n_valid_tiles = tile_end[-1]
    tok = jnp.broadcast_to(jnp.arange(t, dtype=I32)[None, :], (TOP_K, t))
    token_of = jnp.zeros((p_rows,), I32).at[pos.reshape(-1)].set(tok.reshape(-1))
    tile_ids = jnp.arange(n_tiles, dtype=I32)
    tile_expert = jnp.searchsorted(tile_end, jnp.minimum(tile_ids, n_valid_tiles - 1), side="right").astype(I32)
    nv_m = n_valid_tiles.reshape(1).astype(I32)
    nv_g = (n_valid_tiles * (tm // tg)).reshape(1).astype(I32)
    xs = _gather_rows(h, token_of.reshape(p_rows // tg, 1, tg), nv_g, tg)
    act = _gate_up(xs, tile_expert, nv_m, layer, w_gate, w_up, b_gate, b_up, tm)
    y_pairs = _down(act, tile_expert, nv_m, layer, w_down, b_down, tm)
    return _combine(x, y_pairs, pos, wts, mod, layer, final_w, rows_p, seq_s, final_norm=final_norm)


def _even_mixer(x, mod, layer, nw, w_in, q_norm, k_norm, w_out, cache_k, cache_v, dims):
    rows_p, seq_p, n_bp, seq_s, n_bs = dims
    t = x.shape[0]
    qw_, kvw = A_HEADS * HEAD_DIM, A_KV_HEADS * HEAD_DIM
    past = cache_k.shape[1]
    nk_s = past + seq_s
    proj = _norm_matmul(x, 0, nw, w_in.astype(BF16), rows_p, seq_s, mod=mod, layer=layer,
                        m_shift=0, m_scale=1, name="even_in_proj")
    tm_post = _tile(math.gcd(rows_p, seq_s), POST_ROW_TILE)
    cos, sin = _rope_tables(seq_s, HEAD_DIM, tm_post)
    qh, kh, vh, kst = _qk_post(proj, q_norm, k_norm, cos, sin, rows_p, seq_s)
    scale = HEAD_DIM ** -0.5
    group = A_HEADS // A_KV_HEADS
    attn = _attention(qh, kh, vh, n_batch=n_bp, sq=seq_p, nk=seq_p, q_row0=0, k_blk0=0, k_col0=0, v_col0=0,
                      n_kv=A_KV_HEADS, group=group, scale=scale, out_cols=qw_, prev=jnp.zeros((t, qw_), BF16),
                      name="even_attention_ctx")
    k_lat = kh[rows_p:].reshape(n_bs, seq_s, kvw)
    v_lat = vh[rows_p:].reshape(n_bs, seq_s, kvw)
    k_all = jnp.concatenate([cache_k.reshape(n_bs, past, kvw).astype(BF16), k_lat], axis=1).reshape(n_bs * nk_s, kvw)
    v_all = jnp.concatenate([cache_v.reshape(n_bs, past, kvw).astype(BF16), v_lat], axis=1).reshape(n_bs * nk_s, kvw)
    attn = _attention(qh, k_all, v_all, n_batch=n_bs, sq=seq_s, nk=nk_s, q_row0=rows_p, k_blk0=0, k_col0=0,
                      v_col0=0, n_kv=A_KV_HEADS, group=group, scale=scale, out_cols=qw_, prev=attn,
                      name="even_attention_latent")
    ab = _dft_width(proj, (qw_ + 2 * kvw) // B_GROUP_W, rows_p, seq_s)
    four = _dft_seq(ab, n_bp, seq_p, 0, jnp.zeros((t, B_GROUPS * B_GROUP_W), BF16), "fourier_seq_dft_ctx")
    four = _dft_seq(ab, n_bs, seq_s, rows_p, four, "fourier_seq_dft_latent")
    w_out_bf = w_out.astype(BF16)
    x_new = _out_proj(attn, four, w_out_bf[:qw_], w_out_bf[qw_:], x, mod, layer, 2, rows_p, seq_s,
                      name="even_out_proj")
    k_state = kst[:rows_p].reshape(n_bp, seq_p, A_KV_HEADS, HEAD_DIM)
    v_state = proj[:rows_p, qw_ + kvw:qw_ + 2 * kvw].reshape(n_bp, seq_p, A_KV_HEADS, HEAD_DIM)
    return x_new, k_state, v_state


def _odd_mixer(x, mod, layer, nw, w_in, q_a_norm, kv_a_norm, w_uq, w_ukv, conv_w, w_out,
               cache_ckv, cache_kr, dims):
    rows_p, seq_p, n_bp, seq_s, n_bs = dims
    d = x.shape[1]
    past = cache_ckv.shape[1]
    nk_s = past + seq_s
    o1, o2, o3 = Q_LORA, Q_LORA + KV_LORA, Q_LORA + KV_LORA + QK_ROPE
    o4, o5 = o3 + D_CH, o3 + 2 * D_CH
    pad_w = LANE - QK_ROPE
    w_in_r = jnp.concatenate([w_in[:, o3:o4], w_in[:, o4:o5], w_in[:, o5:], w_in[:, o1:o2], w_in[:, :o1],
                              w_in[:, o2:o3], jnp.zeros((d, pad_w), w_in.dtype)], axis=1).astype(BF16)
    c_ckv = 3 * D_CH
    c_cq = c_ckv + KV_LORA
    c_kr = c_cq + Q_LORA
    assert c_ckv % KV_LORA == 0 and c_cq % Q_LORA == 0 and c_kr % LANE == 0
    proj = _norm_matmul(x, 0, nw, w_in_r, rows_p, seq_s, mod=mod, layer=layer, m_shift=0, m_scale=1,
                        name="odd_in_proj")
    hq = QK_NOPE + QK_ROPE
    wq = w_uq.reshape(Q_LORA, C_HEADS, hq)
    wq_rope = jnp.pad(wq[:, :, QK_NOPE:], ((0, 0), (0, 0), (0, pad_w)))
    wq_r = jnp.concatenate([wq[:, :, :QK_NOPE].reshape(Q_LORA, C_HEADS * QK_NOPE),
                            wq_rope.reshape(Q_LORA, C_HEADS * LANE)], axis=1).astype(BF16)
    tm = _tile(math.gcd(rows_p, seq_s), ROW_TILE)
    tn = _tile(wq_r.shape[1], COL_TILE, LANE)
    assert (C_HEADS * QK_NOPE) % tn == 0
    cos_q, sin_q = _rope_tables(seq_s, QK_ROPE, tm)
    q = _norm_matmul(proj, c_cq // Q_LORA, q_a_norm, wq_r, rows_p, seq_s,
                     rope=(cos_q, sin_q, QK_ROPE // 4, (C_HEADS * QK_NOPE) // tn), out_dtype=BF16,
                     name="odd_q_up_proj")
    tm_post = _tile(math.gcd(rows_p, seq_s), POST_ROW_TILE)
    cos_k, sin_k = _rope_tables(seq_s, QK_ROPE, tm_post)
    ckvn, ckvn_bf, kr = _kv_post(proj, c_ckv // KV_LORA, c_kr // LANE, kv_a_norm, cos_k, sin_k, rows_p, seq_s)
    hkv = QK_NOPE + V_HEAD
    wkv = w_ukv.reshape(KV_LORA, C_HEADS, hkv)
    wkv_r = jnp.concatenate([wkv[:, :, :QK_NOPE].reshape(KV_LORA, C_HEADS * QK_NOPE),
                             wkv[:, :, QK_NOPE:].reshape(KV_LORA, C_HEADS * V_HEAD)], axis=1).astype(BF16)
    n_kv_cols = wkv_r.shape[1]
    ckv_all = jnp.concatenate([cache_ckv.astype(BF16), ckvn_bf[rows_p:].reshape(n_bs, seq_s, KV_LORA)],
                              axis=1).reshape(n_bs * nk_s, KV_LORA)
    kr_cache = jnp.pad(cache_kr, ((0, 0), (0, 0), (0, pad_w))).astype(BF16)
    kr_all = jnp.concatenate([kr_cache, kr[rows_p:].reshape(n_bs, seq_s, LANE)], axis=1).reshape(n_bs * nk_s, LANE)

    def kv_up(lat, name):
        m = lat.shape[0]
        tmm = _tile(m, ROW_TILE)
        tnn = _tile(n_kv_cols, 2 * COL_TILE, LANE)
        return _matmul(lat, wkv_r, grid=(1, m // tmm, n_kv_cols // tnn, 1),
                       a_spec=pl.BlockSpec((tmm, KV_LORA), lambda b, i, j, kk: (i, 0)),
                       b_spec=pl.BlockSpec((KV_LORA, tnn), lambda b, i, j, kk: (0, j)),
                       o_spec=pl.BlockSpec((tmm, tnn), lambda b, i, j, kk: (i, j)),
                       out_shape=jax.ShapeDtypeStruct((m, n_kv_cols), BF16),
                       tile=(tmm, tnn, KV_LORA), name=name)

    kv_p = kv_up(ckvn_bf[:rows_p], "odd_kv_up_proj_ctx")
    kv_s = kv_up(ckv_all, "odd_kv_up_proj_latent")
    scale = hq ** -0.5
    out_cols = C_HEADS * V_HEAD
    attn = _attention(q, kv_p, kv_p, n_batch=n_bp, sq=seq_p, nk=seq_p, q_row0=0, k_blk0=0, k_col0=0,
                      v_col0=C_HEADS, n_kv=C_HEADS, group=1, scale=scale, out_cols=out_cols,
                      q2=q, q2_col0=C_HEADS, k2=kr, k2_blk0=0, prev=jnp.zeros((x.shape[0], out_cols), BF16),
                      name="odd_attention_ctx")
    attn = _attention(q, kv_s, kv_s, n_batch=n_bs, sq=seq_s, nk=nk_s, q_row0=rows_p, k_blk0=0, k_col0=0,
                      v_col0=C_HEADS, n_kv=C_HEADS, group=1, scale=scale, out_cols=out_cols,
                      q2=q, q2_col0=C_HEADS, k2=kr_all, k2_blk0=0, prev=attn, name="odd_attention_latent")
    conv = _gated_conv(proj, 0, 1, 2, conv_w, rows_p, seq_p, seq_s)
    w_out_bf = w_out.astype(BF16)
    x_new = _out_proj(attn, conv, w_out_bf[:out_cols], w_out_bf[out_cols:], x, mod, layer, 2, rows_p, seq_s,
                      name="odd_out_proj")
    ckv_state = ckvn[:rows_p].reshape(n_bp, seq_p, KV_LORA)
    kr_state = proj[:rows_p, c_kr:c_kr + QK_ROPE].reshape(n_bp, seq_p, QK_ROPE)
    return x_new, ckv_state, kr_state


def kernel(x_prompt, x_sample, cache_attn_k, cache_attn_v, cache_mla_ckv, cache_mla_krope, c, c_ctx,
           norm1_w, norm2_w, w_ada, b_ada,
           even_w_in, even_q_norm, even_k_norm, even_w_out,
           odd_w_in, odd_q_a_norm, odd_kv_a_norm, odd_w_uq, odd_w_ukv, odd_conv_w, odd_w_out,
           router_w, router_b, w_gate, b_gate, w_up, b_up, w_down, b_down, final_norm_w):
    n_bp, seq_p, d = x_prompt.shape
    n_bs, seq_s, _ = x_sample.shape
    depth = w_ada.shape[0]
    rows_p = n_bp * seq_p
    dims = (rows_p, seq_p, n_bp, seq_s, n_bs)
    n_groups = 1 + n_bs
    g8 = -(-n_groups // SUBLANE) * SUBLANE
    cond = jnp.concatenate([c_ctx[None, :], c, jnp.zeros((g8 - n_groups, d), F32)], axis=0)
    ada = _ada_modulation(cond, w_ada, b_ada)
    mod = ada[:, :n_groups].reshape(depth, n_groups, N_MOD, d).transpose(0, 2, 1, 3)[:, :, :, None, :]
    x = jnp.concatenate([x_prompt.reshape(rows_p, d), x_sample.reshape(n_bs * seq_s, d)], axis=0)
    st_k, st_v, st_ckv, st_kr = [], [], [], []
    for l in range(depth):
        j = l // 2
        if l % 2 == 0:
            x, ks, vs = _even_mixer(x, mod, l, norm1_w[l], even_w_in[j], even_q_norm[j], even_k_norm[j],
                                    even_w_out[j], cache_attn_k[:, j], cache_attn_v[:, j], dims)
            st_k.append(ks)
            st_v.append(vs)
        else:
            x, cs, rs = _odd_mixer(x, mod, l, norm1_w[l], odd_w_in[j], odd_q_a_norm[j], odd_kv_a_norm[j],
                                   odd_w_uq[j], odd_w_ukv[j], odd_conv_w[j], odd_w_out[j],
                                   cache_mla_ckv[:, j], cache_mla_krope[:, j], dims)
            st_ckv.append(cs)
            st_kr.append(rs)
        x = _moe_layer(x, norm2_w[l], mod, l, router_w[l], router_b[l], w_gate, b_gate, w_up, b_up,
                       w_down, b_down, final_norm_w, rows_p, seq_s, final_norm=(l == depth - 1))
    y_prompt = x[:rows_p].reshape(n_bp, seq_p, d)
    y_sample = x[rows_p:].reshape(n_bs, seq_s, d)
    return (y_prompt, y_sample, jnp.stack(st_k, axis=1), jnp.stack(st_v, axis=1),
            jnp.stack(st_ckv, axis=1), jnp.stack(st_kr, axis=1))
```

```python
import functools
import math

import jax
import jax.numpy as jnp
from jax import lax
from jax.experimental import pallas as pl
from jax.experimental.pallas import tpu as pltpu

F32 = jnp.float32
BF16 = jnp.bfloat16
I32 = jnp.int32

GRID_W = 64
HEAD_DIM = 128
ROPE_THETA = 10000.0
NORM_EPS = 1e-6
A_HEADS = 24
A_KV_HEADS = 6
B_GROUPS = 8
B_GROUP_W = 128
C_HEADS = 24
Q_LORA = 896
KV_LORA = 512
QK_NOPE = 128
QK_ROPE = 64
V_HEAD = 128
D_CH = 1024
CONV_W = 3
N_EXPERTS = 32
TOP_K = 4
EXPERT_FF = 1024
SWIGLU_LIMIT = 7.0
SWIGLU_ALPHA = 1.702
N_MOD = 6
LOG2_E = math.log2(math.e)

LANE = 128
SUBLANE = 8
V7X_VMEM_BYTES = 64 * 2**20
MIB = 2**20

ROW_TILE = 512
COL_TILE = 512
POST_ROW_TILE = 256
ATTN_Q_TILE = 256
ADA_COL_TILE = 512
EXPERT_ROW_TILE = 512
GATE_UP_COL_TILE = 512
GATHER_ROW_TILE = 256
COMBINE_ROW_TILE = 128


def _tile(n, pref, align=SUBLANE):
    t = min(pref, n)
    t -= t % align
    while t > align and n % t:
        t -= align
    assert t >= align and n % t == 0, (n, pref, align)
    return t


def _params(sem, vmem_bytes):
    limit = int(min(max(vmem_bytes, 16 * MIB), V7X_VMEM_BYTES - 6 * MIB))
    return pltpu.CompilerParams(dimension_semantics=sem, vmem_limit_bytes=limit)


def _group_of_row(start, rows_p, seq_s):
    return jnp.where(start < rows_p, 0, 1 + jnp.maximum(start - rows_p, 0) // seq_s)


def _ada_kernel(c_ref, w_ref, b_ref, o_ref):
    c = c_ref[...]
    s = (c * jax.nn.sigmoid(c)).astype(BF16)
    o_ref[...] = jnp.dot(s, w_ref[...].astype(BF16), preferred_element_type=F32) + b_ref[...]


def _ada_modulation(cond, w_ada, b_ada):
    depth, d, n = w_ada.shape
    g8 = cond.shape[0]
    tn = _tile(n, ADA_COL_TILE, LANE)
    vmem = 2 * (d * tn * 4) + d * tn * 2 + 4 * MIB
    return pl.pallas_call(
        _ada_kernel,
        grid=(depth, n // tn),
        in_specs=[
            pl.BlockSpec((g8, d), lambda l, j: (0, 0)),
            pl.BlockSpec((None, d, tn), lambda l, j: (l, 0, j)),
            pl.BlockSpec((None, 1, tn), lambda l, j: (l, 0, j)),
        ],
        out_specs=pl.BlockSpec((None, g8, tn), lambda l, j: (l, 0, j)),
        out_shape=jax.ShapeDtypeStruct((depth, g8, n), F32),
        compiler_params=_params(("parallel", "parallel"), vmem),
        name="ada_modulation",
    )(cond, w_ada, b_ada.reshape(depth, 1, n))


def _rmsnorm(x, w):
    return x * lax.rsqrt(jnp.mean(x * x, axis=-1, keepdims=True) + NORM_EPS) * w


def _pack_pair(lo, hi):
    lo_b = lax.bitcast_convert_type(lo.astype(jnp.bfloat16).astype(F32), jnp.uint32)
    hi_b = lax.bitcast_convert_type(hi.astype(jnp.bfloat16).astype(F32), jnp.uint32)
    return hi_b | (lo_b >> 16)


def _unpack_pair(w):
    lo = lax.bitcast_convert_type(w << 16, F32)
    hi = lax.bitcast_convert_type(w & jnp.uint32(0xFFFF0000), F32)
    return lo, hi


def _rope_lanes(y, cos, sin, half):
    lane = lax.broadcasted_iota(I32, y.shape, 1)
    partner = jnp.where(lane % (2 * half) < half,
                        pltpu.roll(y, LANE - half, 1),
                        pltpu.roll(y, half, 1))
    return y * cos + partner * sin


def _rope_tables(n_tokens, rot_dim, ident_rows):
    rows = n_tokens // GRID_W
    row = jnp.repeat(jnp.arange(rows), GRID_W).astype(F32)
    col = jnp.tile(jnp.arange(GRID_W), rows).astype(F32)
    half = rot_dim // 2
    inv = ROPE_THETA ** (-jnp.arange(0, half, 2, dtype=F32) / half)
    ang_r = row[:, None] * inv[None, :]
    ang_c = col[:, None] * inv[None, :]
    ang = jnp.concatenate([ang_r, ang_r, ang_c, ang_c], axis=-1)
    sign = jnp.tile(jnp.concatenate([-jnp.ones((half // 2,), F32), jnp.ones((half // 2,), F32)]), 2)
    cos = jnp.cos(ang)
    sin = jnp.sin(ang) * sign[None, :]
    pad = LANE - rot_dim
    cos = jnp.pad(cos, ((0, ident_rows), (0, pad)), constant_values=1.0)
    sin = jnp.pad(sin, ((0, ident_rows), (0, pad)), constant_values=0.0)
    return cos, sin


def _norm_matmul_kernel(*refs, modulate, rope_half):
    it = iter(refs)
    x_ref, nw_ref = next(it), next(it)
    sc_ref = sh_ref = cos_ref = sin_ref = None
    if modulate:
        sc_ref, sh_ref = next(it), next(it)
    w_ref = next(it)
    if rope_half:
        cos_ref, sin_ref = next(it), next(it)
    o_ref, h_ref = next(it), next(it)

    @pl.when(pl.program_id(1) == 0)
    def _():
        y = _rmsnorm(x_ref[...], nw_ref[...])
        if modulate:
            y = y * (1.0 + sc_ref[...]) + sh_ref[...]
        h_ref[...] = y.astype(BF16)

    acc = jnp.dot(h_ref[...], w_ref[...], preferred_element_type=F32)
    if rope_half:
        cos, sin = cos_ref[...], sin_ref[...]
        for c in range(acc.shape[1] // LANE):
            sl = slice(c * LANE, (c + 1) * LANE)
            o_ref[:, sl] = _rope_lanes(acc[:, sl], cos, sin, rope_half).astype(o_ref.dtype)
    else:
        o_ref[...] = acc.astype(o_ref.dtype)


def _norm_matmul(x, x_col, nw, w, rows_p, seq_s, *, mod=None, layer=0, m_shift=0, m_scale=0,
                 rope=None, out_dtype=F32, name):
    t = x.shape[0]
    k, n = w.shape
    tm = _tile(math.gcd(rows_p, seq_s), ROW_TILE)
    tn = _tile(n, COL_TILE, LANE)
    p_tiles = rows_p // tm
    s_tiles = seq_s // tm

    def grp(i):
        return _group_of_row(i * tm, rows_p, seq_s)

    in_specs = [pl.BlockSpec((tm, k), lambda i, j: (i, x_col)),
                pl.BlockSpec((1, k), lambda i, j: (0, 0))]
    args = [x, nw.reshape(1, k)]
    if mod is not None:
        in_specs += [pl.BlockSpec((None, None, None, 1, k), lambda i, j: (layer, m_scale, grp(i), 0, 0)),
                     pl.BlockSpec((None, None, None, 1, k), lambda i, j: (layer, m_shift, grp(i), 0, 0))]
        args += [mod, mod]
    in_specs.append(pl.BlockSpec((k, tn), lambda i, j: (0, j)))
    args.append(w)
    rope_half = 0
    if rope is not None:
        cos, sin, rope_half, first_tile = rope

        def pos_tile(i, j):
            is_pos = jnp.logical_and(i >= p_tiles, j >= first_tile)
            return jnp.where(is_pos, jnp.maximum(i - p_tiles, 0) % s_tiles, s_tiles)

        in_specs += [pl.BlockSpec((tm, LANE), lambda i, j: (pos_tile(i, j), 0)),
                     pl.BlockSpec((tm, LANE), lambda i, j: (pos_tile(i, j), 0))]
        args += [cos, sin]
    osz = jnp.dtype(out_dtype).itemsize
    vmem = 2 * (tm * k * 4 + k * tn * 2 + tm * tn * osz) + tm * k * 2 + 3 * tm * max(k, tn) * 4 + 4 * MIB
    return pl.pallas_call(
        functools.partial(_norm_matmul_kernel, modulate=mod is not None, rope_half=rope_half),
        grid=(t // tm, n // tn),
        in_specs=in_specs,
        out_specs=pl.BlockSpec((tm, tn), lambda i, j: (i, j)),
        out_shape=jax.ShapeDtypeStruct((t, n), out_dtype),
        scratch_shapes=[pltpu.VMEM((tm, k), BF16)],
        compiler_params=_params(("parallel", "arbitrary"), vmem),
        name=name,
    )(*args)


def _matmul_kernel(*refs, scale, aliased):
    a_ref, b_ref = refs[0], refs[1]
    o_ref, acc_ref = refs[-2], refs[-1]
    del aliased
    kk = pl.program_id(3)

    @pl.when(kk == 0)
    def _():
        acc_ref[...] = jnp.zeros_like(acc_ref)

    acc_ref[...] += jnp.dot(a_ref[...], b_ref[...], preferred_element_type=F32)

    @pl.when(kk == pl.num_programs(3) - 1)
    def _():
        acc = acc_ref[...]
        if scale != 1.0:
            acc = acc * scale
        o_ref[...] = acc.astype(o_ref.dtype)


def _matmul(a, b, *, grid, a_spec, b_spec, o_spec, out_shape, tile, scale=1.0, prev=None, name):
    tm, tn, tk = tile
    in_specs = [a_spec, b_spec]
    args = [a, b]
    aliases = {}
    if prev is not None:
        in_specs.append(pl.BlockSpec(memory_space=pl.ANY))
        args.append(prev)
        aliases = {2: 0}
    osz = jnp.dtype(out_shape.dtype).itemsize
    vmem = 2 * (tm * tk * 2 + tk * tn * 2 + tm * tn * osz) + 2 * tm * tn * 4 + 4 * MIB
    return pl.pallas_call(
        functools.partial(_matmul_kernel, scale=scale, aliased=prev is not None),
        grid=grid,
        in_specs=in_specs,
        out_specs=o_spec,
        out_shape=out_shape,
        scratch_shapes=[pltpu.VMEM((tm, tn), F32)],
        input_output_aliases=aliases,
        compiler_params=_params(("parallel", "parallel", "parallel", "arbitrary"), vmem),
        name=name,
    )(*args)


def _out_proj_kernel(a1_ref, a2_ref, w1_ref, w2_ref, x_ref, g_ref, o_ref):
    acc = jnp.dot(a1_ref[...], w1_ref[...], preferred_element_type=F32)
    acc += jnp.dot(a2_ref[...], w2_ref[...], preferred_element_type=F32)
    o_ref[...] = x_ref[...] + g_ref[...] * acc


def _out_proj(a1, a2, w1, w2, x, mod, layer, m_gate, rows_p, seq_s, *, name):
    t, d = x.shape
    k1, k2 = a1.shape[1], a2.shape[1]
    tm = _tile(math.gcd(rows_p, seq_s), ROW_TILE)
    tn = _tile(d, 2 * COL_TILE, LANE)

    def grp(i):
        return _group_of_row(i * tm, rows_p, seq_s)

    vmem = 2 * (tm * (k1 + k2) * 2 + (k1 + k2) * tn * 2 + 2 * tm * tn * 4) + 2 * tm * tn * 4 + 4 * MIB
    return pl.pallas_call(
        _out_proj_kernel,
        grid=(t // tm, d // tn),
        in_specs=[
            pl.BlockSpec((tm, k1), lambda i, j: (i, 0)),
            pl.BlockSpec((tm, k2), lambda i, j: (i, 0)),
            pl.BlockSpec((k1, tn), lambda i, j: (0, j)),
            pl.BlockSpec((k2, tn), lambda i, j: (0, j)),
            pl.BlockSpec((tm, tn), lambda i, j: (i, j)),
            pl.BlockSpec((None, None, None, 1, tn), lambda i, j: (layer, m_gate, grp(i), 0, j)),
        ],
        out_specs=pl.BlockSpec((tm, tn), lambda i, j: (i, j)),
        out_shape=jax.ShapeDtypeStruct((t, d), F32),
        compiler_params=_params(("parallel", "arbitrary"), vmem),
        name=name,
    )(a1, a2, w1, w2, x, mod)


def _qk_post_kernel(q_ref, k_ref, v_ref, qw_ref, kw_ref, cos_ref, sin_ref,
                    qo_ref, ko_ref, vo_ref, ks_ref):
    cos, sin = cos_ref[...], sin_ref[...]
    qw, kw = qw_ref[...], kw_ref[...]
    for h in range(q_ref.shape[1] // HEAD_DIM):
        sl = slice(h * HEAD_DIM, (h + 1) * HEAD_DIM)
        y = _rope_lanes(_rmsnorm(q_ref[:, sl], qw), cos, sin, HEAD_DIM // 4)
        qo_ref[:, sl] = y.astype(BF16)
    for h in range(k_ref.shape[1] // HEAD_DIM):
        sl = slice(h * HEAD_DIM, (h + 1) * HEAD_DIM)
        y = _rope_lanes(_rmsnorm(k_ref[:, sl], kw), cos, sin, HEAD_DIM // 4)
        ks_ref[:, sl] = y
        ko_ref[:, sl] = y.astype(BF16)
    vo_ref[...] = v_ref[...].astype(BF16)


def _qk_post(proj, q_norm, k_norm, cos, sin, rows_p, seq_s):
    t = proj.shape[0]
    qw_, kvw = A_HEADS * HEAD_DIM, A_KV_HEADS * HEAD_DIM
    tm = _tile(math.gcd(rows_p, seq_s), POST_ROW_TILE)
    p_tiles, s_tiles = rows_p // tm, seq_s // tm
    kblk = qw_ // kvw

    def pos_tile(i):
        return jnp.where(i >= p_tiles, jnp.maximum(i - p_tiles, 0) % s_tiles, s_tiles)

    vmem = 2 * (tm * (qw_ + 2 * kvw) * 4 + tm * (qw_ + 2 * kvw) * 2 + tm * kvw * 4) + 4 * tm * qw_ * 4 + 4 * MIB
    return pl.pallas_call(
        _qk_post_kernel,
        grid=(t // tm,),
        in_specs=[
            pl.BlockSpec((tm, qw_), lambda i: (i, 0)),
            pl.BlockSpec((tm, kvw), lambda i: (i, kblk)),
            pl.BlockSpec((tm, kvw), lambda i: (i, kblk + 1)),
            pl.BlockSpec((1, HEAD_DIM), lambda i: (0, 0)),
            pl.BlockSpec((1, HEAD_DIM), lambda i: (0, 0)),
            pl.BlockSpec((tm, LANE), lambda i: (pos_tile(i), 0)),
            pl.BlockSpec((tm, LANE), lambda i: (pos_tile(i), 0)),
        ],
        out_specs=[
            pl.BlockSpec((tm, qw_), lambda i: (i, 0)),
            pl.BlockSpec((tm, kvw), lambda i: (i, 0)),
            pl.BlockSpec((tm, kvw), lambda i: (i, 0)),
            pl.BlockSpec((tm, kvw), lambda i: (i, 0)),
        ],
        out_shape=[
            jax.ShapeDtypeStruct((t, qw_), BF16),
            jax.ShapeDtypeStruct((t, kvw), BF16),
            jax.ShapeDtypeStruct((t, kvw), BF16),
            jax.ShapeDtypeStruct((t, kvw), F32),
        ],
        compiler_params=_params(("parallel",), vmem),
        name="even_qk_norm_rope",
    )(proj, proj, proj, q_norm.reshape(1, HEAD_DIM), k_norm.reshape(1, HEAD_DIM), cos, sin)


def _attention_kernel(*refs, group, scale, extra, aliased):
    it = iter(refs)
    q_ref, k_ref, v_ref = next(it), next(it), next(it)
    q2_ref = k2_ref = None
    if extra:
        q2_ref, k2_ref = next(it), next(it)
    if aliased:
        next(it)
    o_ref = next(it)
    k = k_ref[...]
    v = v_ref[...]
    if extra:
        k = jnp.concatenate([k, k2_ref[...]], axis=-1)
    for h in range(group):
        sl = slice(h * LANE, (h + 1) * LANE)
        q = q_ref[:, sl]
        if extra:
            q = jnp.concatenate([q, q2_ref[:, sl]], axis=-1)
        s = lax.dot_general(q, k, (((1,), (1,)), ((), ())), preferred_element_type=F32) * (scale * LOG2_E)
        m = jnp.max(s, axis=-1, keepdims=True)
        p = jnp.exp2(s - m)
        den = jnp.sum(p, axis=-1, keepdims=True)
        o = jnp.dot(p.astype(BF16), v, preferred_element_type=F32) / den
        o_ref[:, sl] = o.astype(o_ref.dtype)


def _attention(q, k, v, *, n_batch, sq, nk, q_row0, k_blk0, k_col0, v_col0, n_kv, group, scale,
               out_cols, q2=None, q2_col0=0, k2=None, k2_blk0=0, prev=None, name):
    t = q.shape[0]
    tq = _tile(math.gcd(sq, q_row0), ATTN_Q_TILE)
    qb0 = q_row0 // tq
    qpb = sq // tq
    gw = group * LANE
    extra = q2 is not None
    in_specs = [
        pl.BlockSpec((tq, gw), lambda b, g, i: (qb0 + b * qpb + i, g)),
        pl.BlockSpec((nk, LANE), lambda b, g, i: (k_blk0 + b, k_col0 + g)),
        pl.BlockSpec((nk, LANE), lambda b, g, i: (k_blk0 + b, v_col0 + g)),
    ]
    args = [q, k, v]
    if extra:
        in_specs += [pl.BlockSpec((tq, gw), lambda b, g, i: (qb0 + b * qpb + i, q2_col0 + g)),
                     pl.BlockSpec((nk, LANE), lambda b, g, i: (k2_blk0 + b, 0))]
        args += [q, k2]
    aliases = {}
    if prev is not None:
        in_specs.append(pl.BlockSpec(memory_space=pl.ANY))
        aliases = {len(args): 0}
        args.append(prev)
    kd = 2 * LANE if extra else LANE
    vmem = (2 * (tq * gw * 2 * (2 if extra else 1) + nk * LANE * 2 * (3 if extra else 2) + tq * gw * 2)
            + nk * kd * 2 + 3 * tq * nk * 4 + 4 * MIB)
    return pl.pallas_call(
        functools.partial(_attention_kernel, group=group, scale=scale, extra=extra, aliased=prev is not None),
        grid=(n_batch, n_kv, qpb),
        in_specs=in_specs,
        out_specs=pl.BlockSpec((tq, gw), lambda b, g, i: (qb0 + b * qpb + i, g)),
        out_shape=jax.ShapeDtypeStruct((t, out_cols), BF16),
        input_output_aliases=aliases,
        compiler_params=_params(("parallel", "parallel", "parallel"), vmem),
        name=name,
    )(*args)


def _dft_width_kernel(f_ref, cs_ref, a_ref, b_ref):
    r = jnp.dot(f_ref[...].astype(BF16), cs_ref[...], preferred_element_type=F32)
    a_ref[...] = r[:, :B_GROUP_W].astype(BF16)
    b_ref[...] = r[:, B_GROUP_W:].astype(BF16)


def _dft_tables(n):
    j = jnp.arange(n, dtype=I32)
    jk = (j[:, None] * j[None, :]) % n
    ang = jk.astype(F32) * (2.0 * math.pi / n)
    return jnp.cos(ang), jnp.sin(ang)


def _dft_width(proj, col0_blk, rows_p, seq_s):
    t = proj.shape[0]
    bw = B_GROUPS * B_GROUP_W
    cw, sw = _dft_tables(B_GROUP_W)
    cs = jnp.concatenate([cw, -sw], axis=1).astype(BF16)
    tm = _tile(math.gcd(rows_p, seq_s), 2 * ROW_TILE)
    vmem = 2 * (tm * LANE * 4 + 2 * tm * LANE * 2) + 4 * tm * LANE * 4 + 4 * MIB
    a, b = pl.pallas_call(
        _dft_width_kernel,
        grid=(t // tm, B_GROUPS),
        in_specs=[pl.BlockSpec((tm, B_GROUP_W), lambda i, g: (i, col0_blk + g)),
                  pl.BlockSpec((B_GROUP_W, 2 * B_GROUP_W), lambda i, g: (0, 0))],
        out_specs=[pl.BlockSpec((tm, B_GROUP_W), lambda i, g: (i, g)),
                   pl.BlockSpec((tm, B_GROUP_W), lambda i, g: (i, g))],
        out_shape=[jax.ShapeDtypeStruct((t, bw), BF16), jax.ShapeDtypeStruct((t, bw), BF16)],
        compiler_params=_params(("parallel", "parallel"), vmem),
        name="fourier_width_dft",
    )(proj, cs)
    return jnp.stack([a, b], axis=0)


def _dft_seq(ab, n_batch, seq, row0, prev, name):
    _, t, bw = ab.shape
    c, s = _dft_tables(seq)
    f = jnp.concatenate([c, s], axis=1).astype(BF16)
    tm = _tile(math.gcd(seq, row0), 2 * ROW_TILE)
    tk = tm
    tn = _tile(bw, 2 * COL_TILE, LANE)
    kt = seq // tk
    rb0 = row0 // tk
    ob0 = row0 // tm
    scale = 1.0 / math.sqrt(seq * B_GROUP_W)
    return _matmul(
        f, ab,
        grid=(n_batch, seq // tm, bw // tn, 2 * kt),
        a_spec=pl.BlockSpec((tm, tk), lambda b, i, j, kk: (i, kk)),
        b_spec=pl.BlockSpec((None, tk, tn), lambda b, i, j, kk: (kk // kt, rb0 + b * kt + kk % kt, j)),
        o_spec=pl.BlockSpec((tm, tn), lambda b, i, j, kk: (ob0 + b * (seq // tm) + i, j)),
        out_shape=jax.ShapeDtypeStruct((t, bw), BF16),
        tile=(tm, tn, tk), scale=scale, prev=prev, name=name)


def _kv_post_kernel(ckv_ref, kr_ref, w_ref, cos_ref, sin_ref, cn_ref, cb_ref, kr_out_ref):
    y = _rmsnorm(ckv_ref[...], w_ref[...])
    cn_ref[...] = y
    cb_ref[...] = y.astype(BF16)
    kr_out_ref[...] = _rope_lanes(kr_ref[...], cos_ref[...], sin_ref[...], QK_ROPE // 4).astype(BF16)


def _kv_post(proj, ckv_blk, kr_blk, kv_norm, cos, sin, rows_p, seq_s):
    t = proj.shape[0]
    tm = _tile(math.gcd(rows_p, seq_s), POST_ROW_TILE)
    p_tiles, s_tiles = rows_p // tm, seq_s // tm

    def pos_tile(i):
        return jnp.where(i >= p_tiles, jnp.maximum(i - p_tiles, 0) % s_tiles, s_tiles)

    vmem = 2 * (tm * KV_LORA * 10 + tm * LANE * 14) + 4 * tm * KV_LORA * 4 + 4 * MIB
    return pl.pallas_call(
        _kv_post_kernel,
        grid=(t // tm,),
        in_specs=[
            pl.BlockSpec((tm, KV_LORA), lambda i: (i, ckv_blk)),
            pl.BlockSpec((tm, LANE), lambda i: (i, kr_blk)),
            pl.BlockSpec((1, KV_LORA), lambda i: (0, 0)),
            pl.BlockSpec((tm, LANE), lambda i: (pos_tile(i), 0)),
            pl.BlockSpec((tm, LANE), lambda i: (pos_tile(i), 0)),
        ],
        out_specs=[
            pl.BlockSpec((tm, KV_LORA), lambda i: (i, 0)),
            pl.BlockSpec((tm, KV_LORA), lambda i: (i, 0)),
            pl.BlockSpec((tm, LANE), lambda i: (i, 0)),
        ],
        out_shape=[
            jax.ShapeDtypeStruct((t, KV_LORA), F32),
            jax.ShapeDtypeStruct((t, KV_LORA), BF16),
            jax.ShapeDtypeStruct((t, LANE), BF16),
        ],
        compiler_params=_params(("parallel",), vmem),
        name="odd_kv_norm_rope",
    )(proj, proj, kv_norm.reshape(1, KV_LORA), cos, sin)


def _conv_kernel(u_ref, gb_ref, gc_ref, up_ref, gcp_ref, un_ref, gcn_ref, w_ref, o_ref, *,
                 tr, rows_p, seq_p, seq_s):
    i = pl.program_id(0)
    start = i * tr
    local = jnp.where(start < rows_p, start % seq_p, jnp.maximum(start - rows_p, 0) % seq_s)
    seq = jnp.where(start < rows_p, seq_p, seq_s)
    has_prev = local > 0
    has_next = local + tr < seq
    z = gc_ref[...] * u_ref[...]
    zp = (gcp_ref[...] * up_ref[...])[SUBLANE - 1:SUBLANE, :]
    zn = (gcn_ref[...] * un_ref[...])[0:1, :]
    zp = jnp.where(has_prev, zp, 0.0)
    zn = jnp.where(has_next, zn, 0.0)
    row = lax.broadcasted_iota(I32, z.shape, 0)
    z_prev = jnp.where(row == 0, zp, pltpu.roll(z, 1, 0))
    z_next = jnp.where(row == tr - 1, zn, pltpu.roll(z, tr - 1, 0))
    w = w_ref[...]
    conv = z_prev * w[0:1, :] + z * w[1:2, :] + z_next * w[2:3, :]
    o_ref[...] = (gb_ref[...] * conv).astype(o_ref.dtype)


def _gated_conv(proj, u_blk, gb_blk, gc_blk, conv_w, rows_p, seq_p, seq_s):
    t = proj.shape[0]
    tr = _tile(math.gcd(seq_p, seq_s), POST_ROW_TILE)
    tc = _tile(D_CH, COL_TILE, LANE)
    halo = tr // SUBLANE
    last = t // SUBLANE - 1
    cpb = D_CH // tc

    def main(blk):
        return pl.BlockSpec((tr, tc), lambda i, j: (i, blk * cpb + j))

    def prev(blk):
        return pl.BlockSpec((SUBLANE, tc), lambda i, j: (jnp.maximum(i * halo - 1, 0), blk * cpb + j))

    def nxt(blk):
        return pl.BlockSpec((SUBLANE, tc), lambda i, j: (jnp.minimum((i + 1) * halo, last), blk * cpb + j))

    w8 = jnp.pad(conv_w, ((0, SUBLANE - CONV_W), (0, 0)))
    vmem = 2 * (3 * tr * tc * 4 + 4 * SUBLANE * tc * 4 + tr * tc * 2) + 8 * tr * tc * 4 + 4 * MIB
    return pl.pallas_call(
        functools.partial(_conv_kernel, tr=tr, rows_p=rows_p, seq_p=seq_p, seq_s=seq_s),
        grid=(t // tr, cpb),
        in_specs=[main(u_blk), main(gb_blk), main(gc_blk), prev(u_blk), prev(gc_blk), nxt(u_blk), nxt(gc_blk),
                  pl.BlockSpec((SUBLANE, tc), lambda i, j: (0, j))],
        out_specs=pl.BlockSpec((tr, tc), lambda i, j: (i, j)),
        out_shape=jax.ShapeDtypeStruct((t, D_CH), BF16),
        compiler_params=_params(("parallel", "parallel"), vmem),
        name="odd_gated_conv",
    )(proj, proj, proj, proj, proj, proj, proj, w8)


def _router_kernel(x_ref, nw_ref, sc_ref, sh_ref, rw_ref, rb_ref,
                   h_ref, idx_ref, wt_ref, rank_ref, cnt_ref, carry_ref):
    @pl.when(pl.program_id(0) == 0)
    def _():
        carry_ref[...] = jnp.zeros_like(carry_ref)

    h = _rmsnorm(x_ref[...], nw_ref[...]) * (1.0 + sc_ref[...]) + sh_ref[...]
    slab = h.shape[1] // (2 * LANE)
    for k in range(slab):
        lo = h[:, (2 * k) * LANE:(2 * k + 1) * LANE]
        hi = h[:, (2 * k + 1) * LANE:(2 * k + 2) * LANE]
        h_ref[pl.ds(k, h.shape[0], stride=slab), :] = _pack_pair(lo, hi)
    logits = lax.dot_general(rw_ref[...], h, (((1,), (1,)), ((), ())),
                             precision=lax.Precision.HIGHEST, preferred_element_type=F32) + rb_ref[...]
    n_e, tm = logits.shape
    eidx = lax.broadcasted_iota(I32, logits.shape, 0).astype(F32)
    work = logits
    vals, hots = [], []
    for k in range(TOP_K):
        m = jnp.max(work, axis=0, keepdims=True)
        sel = jnp.min(jnp.where(work == m, eidx, float(n_e)), axis=0, keepdims=True)
        hot = eidx == sel
        idx_ref[k:k + 1, :] = sel.astype(I32)
        vals.append(m)
        hots.append(hot)
        work = jnp.where(hot, -jnp.inf, work)
    exps = [jnp.exp(v - vals[0]) for v in vals]
    den = exps[0]
    for e in exps[1:]:
        den = den + e
    for k in range(TOP_K):
        wt_ref[k:k + 1, :] = exps[k] / den
    chosen = hots[0]
    for hot in hots[1:]:
        chosen = jnp.logical_or(chosen, hot)
    chosen = jnp.where(chosen, 1.0, 0.0)
    r = lax.broadcasted_iota(I32, (tm, tm), 0)
    c = lax.broadcasted_iota(I32, (tm, tm), 1)
    upper = jnp.where(r < c, 1.0, 0.0).astype(BF16)
    before = jnp.dot(chosen.astype(BF16), upper, preferred_element_type=F32) + carry_ref[:, 0:1]
    for k in range(TOP_K):
        rk = jnp.sum(jnp.where(hots[k], before, 0.0), axis=0, keepdims=True)
        rank_ref[k:k + 1, :] = rk.astype(I32)
    carry_ref[...] = carry_ref[...] + jnp.sum(chosen, axis=1, keepdims=True)
    cnt_ref[...] = carry_ref[...].astype(I32)


def _router(x, nw, mod, layer, router_w, router_b, rows_p, seq_s):
    t, d = x.shape
    n_e = router_w.shape[1]
    tm = _tile(math.gcd(rows_p, seq_s), ROW_TILE, LANE)
    slab = d // (2 * LANE)

    def grp(i):
        return _group_of_row(i * tm, rows_p, seq_s)

    vmem = 2 * (tm * d * 6 + n_e * d * 4) + 4 * tm * d * 4 + tm * tm * 4 + 4 * MIB
    return pl.pallas_call(
        _router_kernel,
        grid=(t // tm,),
        in_specs=[
            pl.BlockSpec((tm, d), lambda i: (i, 0)),
            pl.BlockSpec((1, d), lambda i: (0, 0)),
            pl.BlockSpec((None, None, None, 1, d), lambda i: (layer, 4, grp(i), 0, 0)),
            pl.BlockSpec((None, None, None, 1, d), lambda i: (layer, 3, grp(i), 0, 0)),
            pl.BlockSpec((n_e, d), lambda i: (0, 0)),
            pl.BlockSpec((n_e, 1), lambda i: (0, 0)),
        ],
        out_specs=[
            pl.BlockSpec((tm * slab, LANE), lambda i: (i, 0)),
            pl.BlockSpec((TOP_K, tm), lambda i: (0, i)),
            pl.BlockSpec((TOP_K, tm), lambda i: (0, i)),
            pl.BlockSpec((TOP_K, tm), lambda i: (0, i)),
            pl.BlockSpec((n_e, LANE), lambda i: (0, 0)),
        ],
        out_shape=[
            jax.ShapeDtypeStruct((t * slab, LANE), jnp.uint32),
            jax.ShapeDtypeStruct((TOP_K, t), I32),
            jax.ShapeDtypeStruct((TOP_K, t), F32),
            jax.ShapeDtypeStruct((TOP_K, t), I32),
            jax.ShapeDtypeStruct((n_e, LANE), I32),
        ],
        scratch_shapes=[pltpu.VMEM((n_e, LANE), F32)],
        compiler_params=_params(("arbitrary",), vmem),
        name="moe_router",
    )(x, nw.reshape(1, d), mod, mod, router_w.T, router_b.reshape(n_e, 1))


def _slab_copy(src_hbm, row, dst, r, slab, sem):
    return pltpu.make_async_copy(src_hbm.at[pl.ds(pl.multiple_of(row * slab, slab), slab), :],
                                 dst.at[pl.ds(pl.multiple_of(r * slab, slab), slab), :], sem)


def _gather_kernel(nv_ref, tok_ref, tok_next_ref, h_hbm, o_ref, buf0, buf1, sem, *, slab):
    i = pl.program_id(0)
    nv = nv_ref[0]
    rows = o_ref.shape[0]
    bufs = (buf0, buf1)

    def start_tile(tok, slot):
        def body(r, carry):
            _slab_copy(h_hbm, tok[0, r], bufs[slot], r, slab, sem.at[slot]).start()
            return carry
        lax.fori_loop(0, rows, body, 0, unroll=8)

    @pl.when(i == 0)
    def _():
        start_tile(tok_ref, 0)

    for par in (0, 1):
        @pl.when(jnp.logical_and(i % 2 == par, i + 1 < nv))
        def _():
            start_tile(tok_next_ref, 1 - par)

        @pl.when(jnp.logical_and(i % 2 == par, i < nv))
        def _():
            def body(r, carry):
                _slab_copy(h_hbm, 0, bufs[par], r, slab, sem.at[par]).wait()
                return carry
            lax.fori_loop(0, rows, body, 0, unroll=8)
            for k in range(slab):
                lo, hi = _unpack_pair(bufs[par][pl.ds(k, rows, stride=slab), :])
                o_ref[:, (2 * k) * LANE:(2 * k + 1) * LANE] = lo.astype(o_ref.dtype)
                o_ref[:, (2 * k + 1) * LANE:(2 * k + 2) * LANE] = hi.astype(o_ref.dtype)

    @pl.when(i >= nv)
    def _():
        o_ref[...] = jnp.zeros_like(o_ref)


def _gather_rows(h_packed, d, token_of, n_valid, tg):
    nt = token_of.shape[0]
    slab = d // (2 * LANE)
    vmem = 2 * tg * slab * LANE * 4 + 2 * tg * d * 2 + 2 * tg * d * 4 + 4 * MIB
    grid_spec = pltpu.PrefetchScalarGridSpec(
        num_scalar_prefetch=1,
        grid=(nt,),
        in_specs=[
            pl.BlockSpec((None, 1, tg), lambda i, nv: (jnp.minimum(i, nv[0] - 1), 0, 0), memory_space=pltpu.SMEM),
            pl.BlockSpec((None, 1, tg), lambda i, nv: (jnp.minimum(i + 1, nv[0] - 1), 0, 0), memory_space=pltpu.SMEM),
            pl.BlockSpec(memory_space=pl.ANY),
        ],
        out_specs=pl.BlockSpec((tg, d), lambda i, nv: (i, 0)),
        scratch_shapes=[pltpu.VMEM((tg * slab, LANE), jnp.uint32), pltpu.VMEM((tg * slab, LANE), jnp.uint32),
                        pltpu.SemaphoreType.DMA((2,))],
    )
    return pl.pallas_call(
        functools.partial(_gather_kernel, slab=slab),
        grid_spec=grid_spec,
        out_shape=jax.ShapeDtypeStruct((nt * tg, d), BF16),
        compiler_params=_params(("arbitrary",), vmem),
        name="moe_dispatch_gather",
    )(n_valid, token_of, token_of, h_packed)


def _gate_up_kernel(te_ref, nv_ref, x_ref, wg_ref, wu_ref, bg_ref, bu_ref, o_ref, wg_bf, wu_bf):
    r = pl.program_id(1)
    e = te_ref[r]
    e_prev = te_ref[jnp.maximum(r - 1, 0)]

    @pl.when(jnp.logical_or(r == 0, e != e_prev))
    def _():
        wg_bf[...] = wg_ref[...].astype(BF16)
        wu_bf[...] = wu_ref[...].astype(BF16)

    @pl.when(r < nv_ref[0])
    def _():
        x = x_ref[...]
        g = jnp.dot(x, wg_bf[...], preferred_element_type=F32) + bg_ref[...]
        u = jnp.dot(x, wu_bf[...], preferred_element_type=F32) + bu_ref[...]
        g = jnp.minimum(g, SWIGLU_LIMIT)
        u = jnp.clip(u, -SWIGLU_LIMIT, SWIGLU_LIMIT)
        o_ref[...] = ((u + 1.0) * g * jax.nn.sigmoid(SWIGLU_ALPHA * g)).astype(o_ref.dtype)

    @pl.when(r >= nv_ref[0])
    def _():
        o_ref[...] = jnp.zeros_like(o_ref)


def _gate_up(xs, tile_expert, n_valid, layer, w_gate, w_up, b_gate, b_up, tm):
    p, d = xs.shape
    depth, n_e, _, ff = w_gate.shape
    nt = p // tm
    nc = _tile(ff, GATE_UP_COL_TILE, LANE)

    def row(c, r, te, nv):
        return jnp.minimum(r, nv[0] - 1)

    vmem = 2 * (tm * d * 2 + 2 * d * nc * 4 + tm * nc * 2) + 2 * d * nc * 2 + 6 * tm * nc * 4 + 4 * MIB
    grid_spec = pltpu.PrefetchScalarGridSpec(
        num_scalar_prefetch=2,
        grid=(ff // nc, nt),
        in_specs=[
            pl.BlockSpec((tm, d), lambda c, r, te, nv: (row(c, r, te, nv), 0)),
            pl.BlockSpec((None, None, d, nc), lambda c, r, te, nv: (layer, te[r], 0, c)),
            pl.BlockSpec((None, None, d, nc), lambda c, r, te, nv: (layer, te[r], 0, c)),
            pl.BlockSpec((None, None, 1, nc), lambda c, r, te, nv: (layer, te[r], 0, c)),
            pl.BlockSpec((None, None, 1, nc), lambda c, r, te, nv: (layer, te[r], 0, c)),
        ],
        out_specs=pl.BlockSpec((tm, nc), lambda c, r, te, nv: (r, c)),
        scratch_shapes=[pltpu.VMEM((d, nc), BF16), pltpu.VMEM((d, nc), BF16)],
    )
    return pl.pallas_call(
        _gate_up_kernel,
        grid_spec=grid_spec,
        out_shape=jax.ShapeDtypeStruct((p, ff), BF16),
        compiler_params=_params(("arbitrary", "arbitrary"), vmem),
        name="moe_gate_up",
    )(tile_expert, n_valid, xs, w_gate, w_up, b_gate.reshape(depth, n_e, 1, ff), b_up.reshape(depth, n_e, 1, ff))


def _down_kernel(te_ref, nv_ref, a_ref, wd_ref, bd_ref, o_ref, wd_bf, *, slab):
    r = pl.program_id(0)
    e = te_ref[r]
    e_prev = te_ref[jnp.maximum(r - 1, 0)]
    rows = a_ref.shape[0]

    @pl.when(jnp.logical_or(r == 0, e != e_prev))
    def _():
        wd_bf[...] = wd_ref[...].astype(BF16)

    @pl.when(r < nv_ref[0])
    def _():
        a = a_ref[...]
        for k in range(slab):
            sl = slice(2 * k * LANE, (2 * k + 2) * LANE)
            y = jnp.dot(a, wd_bf[:, sl], preferred_element_type=F32) + bd_ref[:, sl]
            o_ref[pl.ds(k, rows, stride=slab), :] = _pack_pair(y[:, :LANE], y[:, LANE:])

    @pl.when(r >= nv_ref[0])
    def _():
        o_ref[...] = jnp.zeros_like(o_ref)


def _down(act, tile_expert, n_valid, layer, w_down, b_down, tm):
    p, ff = act.shape
    depth, n_e, _, d = w_down.shape
    nt = p // tm
    slab = d // (2 * LANE)
    vmem = 2 * (tm * ff * 2 + tm * d * 2) + ff * d * 4 + ff * d * 2 + 4 * tm * 2 * LANE * 4 + 4 * MIB
    grid_spec = pltpu.PrefetchScalarGridSpec(
        num_scalar_prefetch=2,
        grid=(nt,),
        in_specs=[
            pl.BlockSpec((tm, ff), lambda r, te, nv: (jnp.minimum(r, nv[0] - 1), 0)),
            pl.BlockSpec((None, None, ff, d), lambda r, te, nv: (layer, te[r], 0, 0), pipeline_mode=pl.Buffered(1)),
            pl.BlockSpec((None, None, 1, d), lambda r, te, nv: (layer, te[r], 0, 0)),
        ],
        out_specs=pl.BlockSpec((tm * slab, LANE), lambda r, te, nv: (r, 0)),
        scratch_shapes=[pltpu.VMEM((ff, d), BF16)],
    )
    return pl.pallas_call(
        functools.partial(_down_kernel, slab=slab),
        grid_spec=grid_spec,
        out_shape=jax.ShapeDtypeStruct((p * slab, LANE), jnp.uint32),
        compiler_params=_params(("arbitrary",), vmem),
        name="moe_down",
    )(tile_expert, n_valid, act, w_down, b_down.reshape(depth, n_e, 1, d))


def _combine_kernel(pos_ref, pos_next_ref, x_ref, wt_ref, g_ref, fw_ref, y_hbm, o_ref, buf0, buf1, sem, *,
                    final_norm, slab):
    i = pl.program_id(0)
    n = pl.num_programs(0)
    rows = x_ref.shape[0]
    bufs = (buf0, buf1)

    def start_tile(pos, slot):
        def body(r, carry):
            for k in range(TOP_K):
                _slab_copy(y_hbm, pos[k, r], bufs[slot], k * rows + r, slab, sem.at[slot]).start()
            return carry
        lax.fori_loop(0, rows, body, 0, unroll=4)

    @pl.when(i == 0)
    def _():
        start_tile(pos_ref, 0)

    for par in (0, 1):
        @pl.when(jnp.logical_and(i % 2 == par, i + 1 < n))
        def _():
            start_tile(pos_next_ref, 1 - par)

        @pl.when(i % 2 == par)
        def _():
            def wait_body(r, carry):
                for k in range(TOP_K):
                    _slab_copy(y_hbm, 0, bufs[par], k * rows + r, slab, sem.at[par]).wait()
                return carry
            lax.fori_loop(0, rows, wait_body, 0, unroll=4)
            wt = wt_ref[...]
            wb = [jnp.broadcast_to(wt[:, k:k + 1], (rows, LANE)) for k in range(TOP_K)]
            for c in range(slab):
                acc_lo = acc_hi = None
                for k in range(TOP_K):
                    lo, hi = _unpack_pair(bufs[par][pl.ds(k * rows * slab + c, rows, stride=slab), :])
                    acc_lo = wb[k] * lo if acc_lo is None else acc_lo + wb[k] * lo
                    acc_hi = wb[k] * hi if acc_hi is None else acc_hi + wb[k] * hi
                s_lo = slice((2 * c) * LANE, (2 * c + 1) * LANE)
                s_hi = slice((2 * c + 1) * LANE, (2 * c + 2) * LANE)
                o_ref[:, s_lo] = x_ref[:, s_lo] + g_ref[:, s_lo] * acc_lo
                o_ref[:, s_hi] = x_ref[:, s_hi] + g_ref[:, s_hi] * acc_hi

    if final_norm:
        o_ref[...] = _rmsnorm(o_ref[...], fw_ref[...])


def _combine(x, y_pairs, pos, wts, mod, layer, final_w, rows_p, seq_s, *, final_norm):
    t, d = x.shape
    tc = _tile(math.gcd(rows_p, seq_s), COMBINE_ROW_TILE)
    nt = t // tc
    pos3 = pos.reshape(TOP_K, nt, tc).transpose(1, 0, 2)

    def grp(i):
        return _group_of_row(i * tc, rows_p, seq_s)

    slab = d // (2 * LANE)
    vmem = 2 * TOP_K * tc * d * 2 + 2 * (2 * tc * d * 4 + tc * LANE * 4) + 3 * tc * d * 4 + 4 * MIB
    return pl.pallas_call(
        functools.partial(_combine_kernel, final_norm=final_norm, slab=slab),
        grid=(nt,),
        in_specs=[
            pl.BlockSpec((None, TOP_K, tc), lambda i: (i, 0, 0), memory_space=pltpu.SMEM),
            pl.BlockSpec((None, TOP_K, tc), lambda i: (jnp.minimum(i + 1, nt - 1), 0, 0), memory_space=pltpu.SMEM),
            pl.BlockSpec((tc, d), lambda i: (i, 0)),
            pl.BlockSpec((tc, TOP_K), lambda i: (i, 0)),
            pl.BlockSpec((None, None, None, 1, d), lambda i: (layer, 5, grp(i), 0, 0)),
            pl.BlockSpec((1, d), lambda i: (0, 0)),
            pl.BlockSpec(memory_space=pl.ANY),
        ],
        out_specs=pl.BlockSpec((tc, d), lambda i: (i, 0)),
        out_shape=jax.ShapeDtypeStruct((t, d), F32),
        scratch_shapes=[pltpu.VMEM((TOP_K * tc * slab, LANE), jnp.uint32),
                        pltpu.VMEM((TOP_K * tc * slab, LANE), jnp.uint32), pltpu.SemaphoreType.DMA((2,))],
        compiler_params=_params(("arbitrary",), vmem),
        name="moe_combine",
    )(pos3, pos3, x, wts.T, mod, final_w.reshape(1, d), y_pairs)


def _moe_layer(x, nw, mod, layer, router_w, router_b, w_gate, b_gate, w_up, b_up, w_down, b_down,
               final_w, rows_p, seq_s, *, final_norm):
    t, d = x.shape
    n_e = router_w.shape[1]
    tm = EXPERT_ROW_TILE
    tg = _tile(tm, GATHER_ROW_TILE)
    h, idx, wts, rank, cnt = _router(x, nw, mod, layer, router_w, router_b, rows_p, seq_s)
    counts = cnt[:, 0]
    tiles_e = (counts + tm - 1) // tm
    e_ids = jnp.arange(n_e, dtype=I32)
    tile_end = jnp.sum(jnp.where(e_ids[None, :] <= e_ids[:, None], tiles_e[None, :], 0), axis=1)
    offs = (tile_end - tiles_e) * tm
    pos = jnp.sum(jnp.where(idx[:, :, None] == e_ids[None, None, :], offs[None, None, :], 0), axis=-1) + rank
    n_tiles = (t * TOP_K + tm - 1) // tm + n_e
    p_rows = n_tiles * tm
    n_valid_tiles = tile_end[-1]
    tok = jnp.broadcast_to(jnp.arange(t, dtype=I32)[None, :], (TOP_K, t))
    token_of = jnp.zeros((p_rows,), I32).at[pos.reshape(-1)].set(tok.reshape(-1))
    tile_ids = jnp.minimum(jnp.arange(n_tiles, dtype=I32), n_valid_tiles - 1)
    tile_expert = jnp.sum((tile_end[None, :] <= tile_ids[:, None]).astype(I32), axis=1)
    nv_m = n_valid_tiles.reshape(1).astype(I32)
    nv_g = (n_valid_tiles * (tm // tg)).reshape(1).astype(I32)
    xs = _gather_rows(h, d, token_of.reshape(p_rows // tg, 1, tg), nv_g, tg)
    act = _gate_up(xs, tile_expert, nv_m, layer, w_gate, w_up, b_gate, b_up, tm)
    y_pairs = _down(act, tile_expert, nv_m, layer, w_down, b_down, tm)
    return _combine(x, y_pairs, pos, wts, mod, layer, final_w, rows_p, seq_s, final_norm=final_norm)


def _even_mixer(x, mod, layer, nw, w_in, q_norm, k_norm, w_out, cache_k, cache_v, dims):
    rows_p, seq_p, n_bp, seq_s, n_bs = dims
    t = x.shape[0]
    qw_, kvw = A_HEADS * HEAD_DIM, A_KV_HEADS * HEAD_DIM
    past = cache_k.shape[1]
    nk_s = past + seq_s
    proj = _norm_matmul(x, 0, nw, w_in.astype(BF16), rows_p, seq_s, mod=mod, layer=layer,
                        m_shift=0, m_scale=1, name="even_in_proj")
    tm_post = _tile(math.gcd(rows_p, seq_s), POST_ROW_TILE)
    cos, sin = _rope_tables(seq_s, HEAD_DIM, tm_post)
    qh, kh, vh, kst = _qk_post(proj, q_norm, k_norm, cos, sin, rows_p, seq_s)
    scale = HEAD_DIM ** -0.5
    group = A_HEADS // A_KV_HEADS
    attn = _attention(qh, kh, vh, n_batch=n_bp, sq=seq_p, nk=seq_p, q_row0=0, k_blk0=0, k_col0=0, v_col0=0,
                      n_kv=A_KV_HEADS, group=group, scale=scale, out_cols=qw_, prev=jnp.zeros((t, qw_), BF16),
                      name="even_attention_ctx")
    k_lat = kh[rows_p:].reshape(n_bs, seq_s, kvw)
    v_lat = vh[rows_p:].reshape(n_bs, seq_s, kvw)
    k_all = jnp.concatenate([cache_k.reshape(n_bs, past, kvw).astype(BF16), k_lat], axis=1).reshape(n_bs * nk_s, kvw)
    v_all = jnp.concatenate([cache_v.reshape(n_bs, past, kvw).astype(BF16), v_lat], axis=1).reshape(n_bs * nk_s, kvw)
    attn = _attention(qh, k_all, v_all, n_batch=n_bs, sq=seq_s, nk=nk_s, q_row0=rows_p, k_blk0=0, k_col0=0,
                      v_col0=0, n_kv=A_KV_HEADS, group=group, scale=scale, out_cols=qw_, prev=attn,
                      name="even_attention_latent")
    ab = _dft_width(proj, (qw_ + 2 * kvw) // B_GROUP_W, rows_p, seq_s)
    four = _dft_seq(ab, n_bp, seq_p, 0, jnp.zeros((t, B_GROUPS * B_GROUP_W), BF16), "fourier_seq_dft_ctx")
    four = _dft_seq(ab, n_bs, seq_s, rows_p, four, "fourier_seq_dft_latent")
    w_out_bf = w_out.astype(BF16)
    x_new = _out_proj(attn, four, w_out_bf[:qw_], w_out_bf[qw_:], x, mod, layer, 2, rows_p, seq_s,
                      name="even_out_proj")
    k_state = kst[:rows_p].reshape(n_bp, seq_p, A_KV_HEADS, HEAD_DIM)
    v_state = proj[:rows_p, qw_ + kvw:qw_ + 2 * kvw].reshape(n_bp, seq_p, A_KV_HEADS, HEAD_DIM)
    return x_new, k_state, v_state


def _odd_mixer(x, mod, layer, nw, w_in, q_a_norm, kv_a_norm, w_uq, w_ukv, conv_w, w_out,
               cache_ckv, cache_kr, dims):
    rows_p, seq_p, n_bp, seq_s, n_bs = dims
    d = x.shape[1]
    past = cache_ckv.shape[1]
    nk_s = past + seq_s
    o1, o2, o3 = Q_LORA, Q_LORA + KV_LORA, Q_LORA + KV_LORA + QK_ROPE
    o4, o5 = o3 + D_CH, o3 + 2 * D_CH
    pad_w = LANE - QK_ROPE
    w_in_r = jnp.concatenate([w_in[:, o3:o4], w_in[:, o4:o5], w_in[:, o5:], w_in[:, o1:o2], w_in[:, :o1],
                              w_in[:, o2:o3], jnp.zeros((d, pad_w), w_in.dtype)], axis=1).astype(BF16)
    c_ckv = 3 * D_CH
    c_cq = c_ckv + KV_LORA
    c_kr = c_cq + Q_LORA
    assert c_ckv % KV_LORA == 0 and c_cq % Q_LORA == 0 and c_kr % LANE == 0
    proj = _norm_matmul(x, 0, nw, w_in_r, rows_p, seq_s, mod=mod, layer=layer, m_shift=0, m_scale=1,
                        name="odd_in_proj")
    hq = QK_NOPE + QK_ROPE
    wq = w_uq.reshape(Q_LORA, C_HEADS, hq)
    wq_rope = jnp.pad(wq[:, :, QK_NOPE:], ((0, 0), (0, 0), (0, pad_w)))
    wq_r = jnp.concatenate([wq[:, :, :QK_NOPE].reshape(Q_LORA, C_HEADS * QK_NOPE),
                            wq_rope.reshape(Q_LORA, C_HEADS * LANE)], axis=1).astype(BF16)
    tm = _tile(math.gcd(rows_p, seq_s), ROW_TILE)
    tn = _tile(wq_r.shape[1], COL_TILE, LANE)
    assert (C_HEADS * QK_NOPE) % tn == 0
    cos_q, sin_q = _rope_tables(seq_s, QK_ROPE, tm)
    q = _norm_matmul(proj, c_cq // Q_LORA, q_a_norm, wq_r, rows_p, seq_s,
                     rope=(cos_q, sin_q, QK_ROPE // 4, (C_HEADS * QK_NOPE) // tn), out_dtype=BF16,
                     name="odd_q_up_proj")
    tm_post = _tile(math.gcd(rows_p, seq_s), POST_ROW_TILE)
    cos_k, sin_k = _rope_tables(seq_s, QK_ROPE, tm_post)
    ckvn, ckvn_bf, kr = _kv_post(proj, c_ckv // KV_LORA, c_kr // LANE, kv_a_norm, cos_k, sin_k, rows_p, seq_s)
    hkv = QK_NOPE + V_HEAD
    wkv = w_ukv.reshape(KV_LORA, C_HEADS, hkv)
    wkv_r = jnp.concatenate([wkv[:, :, :QK_NOPE].reshape(KV_LORA, C_HEADS * QK_NOPE),
                             wkv[:, :, QK_NOPE:].reshape(KV_LORA, C_HEADS * V_HEAD)], axis=1).astype(BF16)
    n_kv_cols = wkv_r.shape[1]
    ckv_all = jnp.concatenate([cache_ckv.astype(BF16), ckvn_bf[rows_p:].reshape(n_bs, seq_s, KV_LORA)],
                              axis=1).reshape(n_bs * nk_s, KV_LORA)
    kr_cache = jnp.pad(cache_kr, ((0, 0), (0, 0), (0, pad_w))).astype(BF16)
    kr_all = jnp.concatenate([kr_cache, kr[rows_p:].reshape(n_bs, seq_s, LANE)], axis=1).reshape(n_bs * nk_s, LANE)

    def kv_up(lat, name):
        m = lat.shape[0]
        tmm = _tile(m, ROW_TILE)
        tnn = _tile(n_kv_cols, 2 * COL_TILE, LANE)
        return _matmul(lat, wkv_r, grid=(1, m // tmm, n_kv_cols // tnn, 1),
                       a_spec=pl.BlockSpec((tmm, KV_LORA), lambda b, i, j, kk: (i, 0)),
                       b_spec=pl.BlockSpec((KV_LORA, tnn), lambda b, i, j, kk: (0, j)),
                       o_spec=pl.BlockSpec((tmm, tnn), lambda b, i, j, kk: (i, j)),
                       out_shape=jax.ShapeDtypeStruct((m, n_kv_cols), BF16),
                       tile=(tmm, tnn, KV_LORA), name=name)

    kv_p = kv_up(ckvn_bf[:rows_p], "odd_kv_up_proj_ctx")
    kv_s = kv_up(ckv_all, "odd_kv_up_proj_latent")
    scale = hq ** -0.5
    out_cols = C_HEADS * V_HEAD
    attn = _attention(q, kv_p, kv_p, n_batch=n_bp, sq=seq_p, nk=seq_p, q_row0=0, k_blk0=0, k_col0=0,
                      v_col0=C_HEADS, n_kv=C_HEADS, group=1, scale=scale, out_cols=out_cols,
                      q2=q, q2_col0=C_HEADS, k2=kr, k2_blk0=0, prev=jnp.zeros((x.shape[0], out_cols), BF16),
                      name="odd_attention_ctx")
    attn = _attention(q, kv_s, kv_s, n_batch=n_bs, sq=seq_s, nk=nk_s, q_row0=rows_p, k_blk0=0, k_col0=0,
                      v_col0=C_HEADS, n_kv=C_HEADS, group=1, scale=scale, out_cols=out_cols,
                      q2=q, q2_col0=C_HEADS, k2=kr_all, k2_blk0=0, prev=attn, name="odd_attention_latent")
    conv = _gated_conv(proj, 0, 1, 2, conv_w, rows_p, seq_p, seq_s)
    w_out_bf = w_out.astype(BF16)
    x_new = _out_proj(attn, conv, w_out_bf[:out_cols], w_out_bf[out_cols:], x, mod, layer, 2, rows_p, seq_s,
                      name="odd_out_proj")
    ckv_state = ckvn[:rows_p].reshape(n_bp, seq_p, KV_LORA)
    kr_state = proj[:rows_p, c_kr:c_kr + QK_ROPE].reshape(n_bp, seq_p, QK_ROPE)
    return x_new, ckv_state, kr_state


def kernel(x_prompt, x_sample, cache_attn_k, cache_attn_v, cache_mla_ckv, cache_mla_krope, c, c_ctx,
           norm1_w, norm2_w, w_ada, b_ada,
           even_w_in, even_q_norm, even_k_norm, even_w_out,
           odd_w_in, odd_q_a_norm, odd_kv_a_norm, odd_w_uq, odd_w_ukv, odd_conv_w, odd_w_out,
           router_w, router_b, w_gate, b_gate, w_up, b_up, w_down, b_down, final_norm_w):
    n_bp, seq_p, d = x_prompt.shape
    n_bs, seq_s, _ = x_sample.shape
    depth = w_ada.shape[0]
    rows_p = n_bp * seq_p
    dims = (rows_p, seq_p, n_bp, seq_s, n_bs)
    n_groups = 1 + n_bs
    g8 = -(-n_groups // SUBLANE) * SUBLANE
    cond = jnp.concatenate([c_ctx[None, :], c, jnp.zeros((g8 - n_groups, d), F32)], axis=0)
    ada = _ada_modulation(cond, w_ada, b_ada)
    mod = ada[:, :n_groups].reshape(depth, n_groups, N_MOD, d).transpose(0, 2, 1, 3)[:, :, :, None, :]
    x = jnp.concatenate([x_prompt.reshape(rows_p, d), x_sample.reshape(n_bs * seq_s, d)], axis=0)
    st_k, st_v, st_ckv, st_kr = [], [], [], []
    for l in range(depth):
        j = l // 2
        if l % 2 == 0:
            x, ks, vs = _even_mixer(x, mod, l, norm1_w[l], even_w_in[j], even_q_norm[j], even_k_norm[j],
                                    even_w_out[j], cache_attn_k[:, j], cache_attn_v[:, j], dims)
            st_k.append(ks)
            st_v.append(vs)
        else:
            x, cs, rs = _odd_mixer(x, mod, l, norm1_w[l], odd_w_in[j], odd_q_a_norm[j], odd_kv_a_norm[j],
                                   odd_w_uq[j], odd_w_ukv[j], odd_conv_w[j], odd_w_out[j],
                                   cache_mla_ckv[:, j], cache_mla_krope[:, j], dims)
            st_ckv.append(cs)
            st_kr.append(rs)
        x = _moe_layer(x, norm2_w[l], mod, l, router_w[l], router_b[l], w_gate, b_gate, w_up, b_up,
                       w_down, b_down, final_norm_w, rows_p, seq_s, final_norm=(l == depth - 1))
    y_prompt = x[:rows_p].reshape(n_bp, seq_p, d)
    y_sample = x[rows_p:].reshape(n_bs, seq_s, d)
    return (y_prompt, y_sample, jnp.stack(st_k, axis=1), jnp.stack(st_v, axis=1),
            jnp.stack(st_ckv, axis=1), jnp.stack(st_kr, axis=1))
```

```python
import functools
import math

import jax
import jax.numpy as jnp
from jax import lax
from jax.experimental import pallas as pl
from jax.experimental.pallas import tpu as pltpu

F32 = jnp.float32
BF16 = jnp.bfloat16
I32 = jnp.int32

GRID_W = 64
HEAD_DIM = 128
ROPE_THETA = 10000.0
NORM_EPS = 1e-6
A_HEADS = 24
A_KV_HEADS = 6
B_GROUPS = 8
B_GROUP_W = 128
C_HEADS = 24
Q_LORA = 896
KV_LORA = 512
QK_NOPE = 128
QK_ROPE = 64
V_HEAD = 128
D_CH = 1024
CONV_W = 3
N_EXPERTS = 32
TOP_K = 4
EXPERT_FF = 1024
SWIGLU_LIMIT = 7.0
SWIGLU_ALPHA = 1.702
N_MOD = 6
LOG2_E = math.log2(math.e)

LANE = 128
SUBLANE = 8
D_MODEL = 4096
SLAB_ROWS = D_MODEL // (2 * LANE)
V7X_VMEM_BYTES = 64 * 2**20
MIB = 2**20

ROW_TILE = 512
COL_TILE = 512
IN_PROJ_COL_TILE = 1408
NORM_ROW_CHUNK = 64
POST_ROW_TILE = 256
ATTN_Q_TILE = 256
ADA_COL_TILE = 512
EXPERT_ROW_TILE = 512
GATE_UP_COL_TILE = 512
GATHER_ROW_TILE = 256
COMBINE_ROW_TILE = 128


def _tile(n, pref, align=SUBLANE):
    t = min(pref, n)
    t -= t % align
    while t > align and n % t:
        t -= align
    assert t >= align and n % t == 0, (n, pref, align)
    return t


def _params(sem, vmem_bytes):
    limit = int(min(max(vmem_bytes, 16 * MIB), V7X_VMEM_BYTES - 6 * MIB))
    return pltpu.CompilerParams(dimension_semantics=sem, vmem_limit_bytes=limit)


def _group_of_row(start, rows_p, seq_s):
    return jnp.where(start < rows_p, 0, 1 + jnp.maximum(start - rows_p, 0) // seq_s)


def _ada_kernel(c_ref, w_ref, b_ref, o_ref):
    c = c_ref[...]
    s = (c * jax.nn.sigmoid(c)).astype(BF16)
    o_ref[...] = jnp.dot(s, w_ref[...].astype(BF16), preferred_element_type=F32) + b_ref[...]


def _ada_modulation(cond, w_ada, b_ada):
    depth, d, n = w_ada.shape
    g8 = cond.shape[0]
    tn = _tile(n, ADA_COL_TILE, LANE)
    vmem = 2 * (d * tn * 4) + d * tn * 2 + 4 * MIB
    return pl.pallas_call(
        _ada_kernel,
        grid=(depth, n // tn),
        in_specs=[
            pl.BlockSpec((g8, d), lambda l, j: (0, 0)),
            pl.BlockSpec((None, d, tn), lambda l, j: (l, 0, j)),
            pl.BlockSpec((None, 1, tn), lambda l, j: (l, 0, j)),
        ],
        out_specs=pl.BlockSpec((None, g8, tn), lambda l, j: (l, 0, j)),
        out_shape=jax.ShapeDtypeStruct((depth, g8, n), F32),
        compiler_params=_params(("parallel", "parallel"), vmem),
        name="ada_modulation",
    )(cond, w_ada, b_ada.reshape(depth, 1, n))


def _rmsnorm(x, w):
    return x * lax.rsqrt(jnp.mean(x * x, axis=-1, keepdims=True) + NORM_EPS) * w


def _pack_pair(lo, hi):
    lo_b = lax.bitcast_convert_type(lo.astype(jnp.bfloat16).astype(F32), jnp.uint32)
    hi_b = lax.bitcast_convert_type(hi.astype(jnp.bfloat16).astype(F32), jnp.uint32)
    return hi_b | (lo_b >> 16)


def _packed_width(d):
    assert d == 2 * LANE * SLAB_ROWS, (d, SLAB_ROWS)
    return LANE


def _pair_position(q, width):
    assert width == LANE
    return q, slice(0, LANE)


def _unpack_pair(w):
    lo = lax.bitcast_convert_type(w << 16, F32)
    hi = lax.bitcast_convert_type(w & jnp.uint32(0xFFFF0000), F32)
    return lo, hi


def _rope_lanes(y, cos, sin, half):
    lane = lax.broadcasted_iota(I32, y.shape, 1)
    partner = jnp.where(lane % (2 * half) < half,
                        pltpu.roll(y, LANE - half, 1),
                        pltpu.roll(y, half, 1))
    return y * cos + partner * sin


def _rope_tables(n_tokens, rot_dim, ident_rows):
    rows = n_tokens // GRID_W
    row = jnp.repeat(jnp.arange(rows), GRID_W).astype(F32)
    col = jnp.tile(jnp.arange(GRID_W), rows).astype(F32)
    half = rot_dim // 2
    inv = ROPE_THETA ** (-jnp.arange(0, half, 2, dtype=F32) / half)
    ang_r = row[:, None] * inv[None, :]
    ang_c = col[:, None] * inv[None, :]
    ang = jnp.concatenate([ang_r, ang_r, ang_c, ang_c], axis=-1)
    sign = jnp.tile(jnp.concatenate([-jnp.ones((half // 2,), F32), jnp.ones((half // 2,), F32)]), 2)
    cos = jnp.cos(ang)
    sin = jnp.sin(ang) * sign[None, :]
    pad = LANE - rot_dim
    cos = jnp.pad(cos, ((0, ident_rows), (0, pad)), constant_values=1.0)
    sin = jnp.pad(sin, ((0, ident_rows), (0, pad)), constant_values=0.0)
    return cos, sin


def _norm_matmul_kernel(*refs, modulate, rope_half):
    it = iter(refs)
    x_ref, nw_ref = next(it), next(it)
    sc_ref = sh_ref = cos_ref = sin_ref = None
    if modulate:
        sc_ref, sh_ref = next(it), next(it)
    w_ref = next(it)
    if rope_half:
        cos_ref, sin_ref = next(it), next(it)
    o_ref, h_ref = next(it), next(it)

    @pl.when(pl.program_id(1) == 0)
    def _():
        rc = _tile(x_ref.shape[0], NORM_ROW_CHUNK, 2 * SUBLANE)

        def chunk(r, carry):
            rows = pl.ds(pl.multiple_of(r * rc, rc), rc)
            y = _rmsnorm(x_ref[rows, :], nw_ref[...])
            if modulate:
                y = y * (1.0 + sc_ref[...]) + sh_ref[...]
            h_ref[rows, :] = y.astype(BF16)
            return carry
        lax.fori_loop(0, x_ref.shape[0] // rc, chunk, 0)

    acc = jnp.dot(h_ref[...], w_ref[...], preferred_element_type=F32)
    if rope_half:
        cos, sin = cos_ref[...], sin_ref[...]
        for c in range(acc.shape[1] // LANE):
            sl = slice(c * LANE, (c + 1) * LANE)
            o_ref[:, sl] = _rope_lanes(acc[:, sl], cos, sin, rope_half).astype(o_ref.dtype)
    else:
        o_ref[...] = acc.astype(o_ref.dtype)


def _norm_matmul(x, x_col, nw, w, rows_p, seq_s, *, mod=None, layer=0, m_shift=0, m_scale=0,
                 rope=None, out_dtype=F32, name):
    t = x.shape[0]
    k, n = w.shape
    tm = _tile(math.gcd(rows_p, seq_s), ROW_TILE)
    tn = _tile(n, IN_PROJ_COL_TILE, LANE)
    p_tiles = rows_p // tm
    s_tiles = seq_s // tm

    def grp(i):
        return _group_of_row(i * tm, rows_p, seq_s)

    in_specs = [pl.BlockSpec((tm, k), lambda i, j: (i, x_col)),
                pl.BlockSpec((1, k), lambda i, j: (0, 0))]
    args = [x, nw.reshape(1, k)]
    if mod is not None:
        in_specs += [pl.BlockSpec((None, None, None, 1, k), lambda i, j: (layer, m_scale, grp(i), 0, 0)),
                     pl.BlockSpec((None, None, None, 1, k), lambda i, j: (layer, m_shift, grp(i), 0, 0))]
        args += [mod, mod]
    in_specs.append(pl.BlockSpec((k, tn), lambda i, j: (0, j)))
    args.append(w)
    rope_half = 0
    if rope is not None:
        cos, sin, rope_half, first_tile = rope

        def pos_tile(i, j):
            is_pos = jnp.logical_and(i >= p_tiles, j >= first_tile)
            return jnp.where(is_pos, jnp.maximum(i - p_tiles, 0) % s_tiles, s_tiles)

        in_specs += [pl.BlockSpec((tm, LANE), lambda i, j: (pos_tile(i, j), 0)),
                     pl.BlockSpec((tm, LANE), lambda i, j: (pos_tile(i, j), 0))]
        args += [cos, sin]
    osz = jnp.dtype(out_dtype).itemsize
    vmem = 2 * (tm * k * 4 + k * tn * 2 + tm * tn * osz) + tm * k * 2 + 2 * tm * tn * 4 + 4 * MIB
    return pl.pallas_call(
        functools.partial(_norm_matmul_kernel, modulate=mod is not None, rope_half=rope_half),
        grid=(t // tm, n // tn),
        in_specs=in_specs,
        out_specs=pl.BlockSpec((tm, tn), lambda i, j: (i, j)),
        out_shape=jax.ShapeDtypeStruct((t, n), out_dtype),
        scratch_shapes=[pltpu.VMEM((tm, k), BF16)],
        compiler_params=_params(("parallel", "arbitrary"), vmem),
        name=name,
    )(*args)


def _matmul_kernel(*refs, scale, aliased):
    a_ref, b_ref = refs[0], refs[1]
    o_ref, acc_ref = refs[-2], refs[-1]
    del aliased
    kk = pl.program_id(3)

    @pl.when(kk == 0)
    def _():
        acc_ref[...] = jnp.zeros_like(acc_ref)

    acc_ref[...] += jnp.dot(a_ref[...], b_ref[...], preferred_element_type=F32)

    @pl.when(kk == pl.num_programs(3) - 1)
    def _():
        acc = acc_ref[...]
        if scale != 1.0:
            acc = acc * scale
        o_ref[...] = acc.astype(o_ref.dtype)


def _matmul(a, b, *, grid, a_spec, b_spec, o_spec, out_shape, tile, scale=1.0, prev=None, name):
    tm, tn, tk = tile
    in_specs = [a_spec, b_spec]
    args = [a, b]
    aliases = {}
    if prev is not None:
        in_specs.append(pl.BlockSpec(memory_space=pl.ANY))
        args.append(prev)
        aliases = {2: 0}
    osz = jnp.dtype(out_shape.dtype).itemsize
    vmem = 2 * (tm * tk * 2 + tk * tn * 2 + tm * tn * osz) + 2 * tm * tn * 4 + 4 * MIB
    return pl.pallas_call(
        functools.partial(_matmul_kernel, scale=scale, aliased=prev is not None),
        grid=grid,
        in_specs=in_specs,
        out_specs=o_spec,
        out_shape=out_shape,
        scratch_shapes=[pltpu.VMEM((tm, tn), F32)],
        input_output_aliases=aliases,
        compiler_params=_params(("parallel", "parallel", "parallel", "arbitrary"), vmem),
        name=name,
    )(*args)


def _out_proj_kernel(a1_ref, a2_ref, w1_ref, w2_ref, x_ref, g_ref, o_ref):
    acc = jnp.dot(a1_ref[...], w1_ref[...], preferred_element_type=F32)
    acc += jnp.dot(a2_ref[...], w2_ref[...], preferred_element_type=F32)
    o_ref[...] = x_ref[...] + g_ref[...] * acc


def _out_proj(a1, a2, w1, w2, x, mod, layer, m_gate, rows_p, seq_s, *, name):
    t, d = x.shape
    k1, k2 = a1.shape[1], a2.shape[1]
    tm = _tile(math.gcd(rows_p, seq_s), ROW_TILE)
    tn = _tile(d, 2 * COL_TILE, LANE)

    def grp(i):
        return _group_of_row(i * tm, rows_p, seq_s)

    vmem = 2 * (tm * (k1 + k2) * 2 + (k1 + k2) * tn * 2 + 2 * tm * tn * 4) + 2 * tm * tn * 4 + 4 * MIB
    return pl.pallas_call(
        _out_proj_kernel,
        grid=(t // tm, d // tn),
        in_specs=[
            pl.BlockSpec((tm, k1), lambda i, j: (i, 0)),
            pl.BlockSpec((tm, k2), lambda i, j: (i, 0)),
            pl.BlockSpec((k1, tn), lambda i, j: (0, j)),
            pl.BlockSpec((k2, tn), lambda i, j: (0, j)),
            pl.BlockSpec((tm, tn), lambda i, j: (i, j)),
            pl.BlockSpec((None, None, None, 1, tn), lambda i, j: (layer, m_gate, grp(i), 0, j)),
        ],
        out_specs=pl.BlockSpec((tm, tn), lambda i, j: (i, j)),
        out_shape=jax.ShapeDtypeStruct((t, d), F32),
        compiler_params=_params(("parallel", "arbitrary"), vmem),
        name=name,
    )(a1, a2, w1, w2, x, mod)


def _qk_post_kernel(q_ref, k_ref, v_ref, qw_ref, kw_ref, cos_ref, sin_ref,
                    qo_ref, ko_ref, vo_ref, ks_ref):
    cos, sin = cos_ref[...], sin_ref[...]
    qw, kw = qw_ref[...], kw_ref[...]
    for h in range(q_ref.shape[1] // HEAD_DIM):
        sl = slice(h * HEAD_DIM, (h + 1) * HEAD_DIM)
        y = _rope_lanes(_rmsnorm(q_ref[:, sl], qw), cos, sin, HEAD_DIM // 4)
        qo_ref[:, sl] = y.astype(BF16)
    for h in range(k_ref.shape[1] // HEAD_DIM):
        sl = slice(h * HEAD_DIM, (h + 1) * HEAD_DIM)
        y = _rope_lanes(_rmsnorm(k_ref[:, sl], kw), cos, sin, HEAD_DIM // 4)
        ks_ref[:, sl] = y
        ko_ref[:, sl] = y.astype(BF16)
    vo_ref[...] = v_ref[...].astype(BF16)


def _qk_post(proj, q_norm, k_norm, cos, sin, rows_p, seq_s):
    t = proj.shape[0]
    qw_, kvw = A_HEADS * HEAD_DIM, A_KV_HEADS * HEAD_DIM
    tm = _tile(math.gcd(rows_p, seq_s), POST_ROW_TILE)
    p_tiles, s_tiles = rows_p // tm, seq_s // tm
    kblk = qw_ // kvw

    def pos_tile(i):
        return jnp.where(i >= p_tiles, jnp.maximum(i - p_tiles, 0) % s_tiles, s_tiles)

    vmem = 2 * (tm * (qw_ + 2 * kvw) * 4 + tm * (qw_ + 2 * kvw) * 2 + tm * kvw * 4) + 4 * tm * qw_ * 4 + 4 * MIB
    return pl.pallas_call(
        _qk_post_kernel,
        grid=(t // tm,),
        in_specs=[
            pl.BlockSpec((tm, qw_), lambda i: (i, 0)),
            pl.BlockSpec((tm, kvw), lambda i: (i, kblk)),
            pl.BlockSpec((tm, kvw), lambda i: (i, kblk + 1)),
            pl.BlockSpec((1, HEAD_DIM), lambda i: (0, 0)),
            pl.BlockSpec((1, HEAD_DIM), lambda i: (0, 0)),
            pl.BlockSpec((tm, LANE), lambda i: (pos_tile(i), 0)),
            pl.BlockSpec((tm, LANE), lambda i: (pos_tile(i), 0)),
        ],
        out_specs=[
            pl.BlockSpec((tm, qw_), lambda i: (i, 0)),
            pl.BlockSpec((tm, kvw), lambda i: (i, 0)),
            pl.BlockSpec((tm, kvw), lambda i: (i, 0)),
            pl.BlockSpec((tm, kvw), lambda i: (i, 0)),
        ],
        out_shape=[
            jax.ShapeDtypeStruct((t, qw_), BF16),
            jax.ShapeDtypeStruct((t, kvw), BF16),
            jax.ShapeDtypeStruct((t, kvw), BF16),
            jax.ShapeDtypeStruct((t, kvw), F32),
        ],
        compiler_params=_params(("parallel",), vmem),
        name="even_qk_norm_rope",
    )(proj, proj, proj, q_norm.reshape(1, HEAD_DIM), k_norm.reshape(1, HEAD_DIM), cos, sin)


def _attention_kernel(*refs, group, scale, extra, aliased):
    it = iter(refs)
    q_ref, k_ref, v_ref = next(it), next(it), next(it)
    q2_ref = k2_ref = None
    if extra:
        q2_ref, k2_ref = next(it), next(it)
    if aliased:
        next(it)
    o_ref = next(it)
    k = k_ref[...]
    v = v_ref[...]
    if extra:
        k = jnp.concatenate([k, k2_ref[...]], axis=-1)
    for h in range(group):
        sl = slice(h * LANE, (h + 1) * LANE)
        q = q_ref[:, sl]
        if extra:
            q = jnp.concatenate([q, q2_ref[:, sl]], axis=-1)
        s = lax.dot_general(q, k, (((1,), (1,)), ((), ())), preferred_element_type=F32) * (scale * LOG2_E)
        m = jnp.max(s, axis=-1, keepdims=True)
        p = jnp.exp2(s - m)
        den = jnp.sum(p, axis=-1, keepdims=True)
        o = jnp.dot(p.astype(BF16), v, preferred_element_type=F32) / den
        o_ref[:, sl] = o.astype(o_ref.dtype)


def _attention(q, k, v, *, n_batch, sq, nk, q_row0, k_blk0, k_col0, v_col0, n_kv, group, scale,
               out_cols, q2=None, q2_col0=0, k2=None, k2_blk0=0, prev=None, name):
    t = q.shape[0]
    tq = _tile(math.gcd(sq, q_row0), ATTN_Q_TILE)
    qb0 = q_row0 // tq
    qpb = sq // tq
    gw = group * LANE
    extra = q2 is not None
    in_specs = [
        pl.BlockSpec((tq, gw), lambda b, g, i: (qb0 + b * qpb + i, g)),
        pl.BlockSpec((nk, LANE), lambda b, g, i: (k_blk0 + b, k_col0 + g)),
        pl.BlockSpec((nk, LANE), lambda b, g, i: (k_blk0 + b, v_col0 + g)),
    ]
    args = [q, k, v]
    if extra:
        in_specs += [pl.BlockSpec((tq, gw), lambda b, g, i: (qb0 + b * qpb + i, q2_col0 + g)),
                     pl.BlockSpec((nk, LANE), lambda b, g, i: (k2_blk0 + b, 0))]
        args += [q, k2]
    aliases = {}
    if prev is not None:
        in_specs.append(pl.BlockSpec(memory_space=pl.ANY))
        aliases = {len(args): 0}
        args.append(prev)
    kd = 2 * LANE if extra else LANE
    vmem = (2 * (tq * gw * 2 * (2 if extra else 1) + nk * LANE * 2 * (3 if extra else 2) + tq * gw * 2)
            + nk * kd * 2 + 3 * tq * nk * 4 + 4 * MIB)
    return pl.pallas_call(
        functools.partial(_attention_kernel, group=group, scale=scale, extra=extra, aliased=prev is not None),
        grid=(n_batch, n_kv, qpb),
        in_specs=in_specs,
        out_specs=pl.BlockSpec((tq, gw), lambda b, g, i: (qb0 + b * qpb + i, g)),
        out_shape=jax.ShapeDtypeStruct((t, out_cols), BF16),
        input_output_aliases=aliases,
        compiler_params=_params(("parallel", "parallel", "parallel"), vmem),
        name=name,
    )(*args)


def _dft_width_kernel(f_ref, cs_ref, ab_ref):
    r = jnp.dot(f_ref[...].astype(BF16), cs_ref[...], preferred_element_type=F32)
    ab_ref[0] = r[:, :B_GROUP_W].astype(BF16)
    ab_ref[1] = r[:, B_GROUP_W:].astype(BF16)


def _dft_matrix(n):
    base = _tile(n, max(SUBLANE, int(math.sqrt(n))))
    k = jnp.arange(n, dtype=I32)

    def tables(rows):
        ang = ((rows[:, None] * k[None, :]) % n).astype(F32) * (2.0 * math.pi / n)
        return jnp.cos(ang), jnp.sin(ang)

    c1, s1 = tables(jnp.arange(n // base, dtype=I32) * base)
    c2, s2 = tables(jnp.arange(base, dtype=I32))
    q1 = jnp.concatenate([c1, s1], axis=1)[:, None, :]
    q2 = jnp.concatenate([-s1, c1], axis=1)[:, None, :]
    r1 = jnp.concatenate([c2, c2], axis=1)[None, :, :]
    r2 = jnp.concatenate([s2, s2], axis=1)[None, :, :]
    return (q1 * r1 + q2 * r2).reshape(n, 2 * n)


def _dft_tables(n):
    j = jnp.arange(n, dtype=I32)
    jk = (j[:, None] * j[None, :]) % n
    ang = jk.astype(F32) * (2.0 * math.pi / n)
    return jnp.cos(ang), jnp.sin(ang)


def _dft_width(proj, col0_blk, rows_p, seq_s):
    t = proj.shape[0]
    bw = B_GROUPS * B_GROUP_W
    cw, sw = _dft_tables(B_GROUP_W)
    cs = jnp.concatenate([cw, -sw], axis=1).astype(BF16)
    tm = _tile(math.gcd(rows_p, seq_s), 2 * ROW_TILE)
    vmem = 2 * (tm * LANE * 4 + 2 * tm * LANE * 2) + 4 * tm * LANE * 4 + 4 * MIB
    return pl.pallas_call(
        _dft_width_kernel,
        grid=(t // tm, B_GROUPS),
        in_specs=[pl.BlockSpec((tm, B_GROUP_W), lambda i, g: (i, col0_blk + g)),
                  pl.BlockSpec((B_GROUP_W, 2 * B_GROUP_W), lambda i, g: (0, 0))],
        out_specs=pl.BlockSpec((2, tm, B_GROUP_W), lambda i, g: (0, i, g)),
        out_shape=jax.ShapeDtypeStruct((2, t, bw), BF16),
        compiler_params=_params(("parallel", "parallel"), vmem),
        name="fourier_width_dft",
    )(proj, cs)


def _dft_seq(ab, n_batch, seq, row0, prev, name):
    _, t, bw = ab.shape
    f = _dft_matrix(seq).astype(BF16)
    tm = _tile(math.gcd(seq, row0), 2 * ROW_TILE)
    tk = tm
    tn = _tile(bw, 2 * COL_TILE, LANE)
    kt = seq // tk
    rb0 = row0 // tk
    ob0 = row0 // tm
    scale = 1.0 / math.sqrt(seq * B_GROUP_W)
    return _matmul(
        f, ab,
        grid=(n_batch, seq // tm, bw // tn, 2 * kt),
        a_spec=pl.BlockSpec((tm, tk), lambda b, i, j, kk: (i, kk)),
        b_spec=pl.BlockSpec((None, tk, tn), lambda b, i, j, kk: (kk // kt, rb0 + b * kt + kk % kt, j)),
        o_spec=pl.BlockSpec((tm, tn), lambda b, i, j, kk: (ob0 + b * (seq // tm) + i, j)),
        out_shape=jax.ShapeDtypeStruct((t, bw), BF16),
        tile=(tm, tn, tk), scale=scale, prev=prev, name=name)


def _kv_post_kernel(ckv_ref, kr_ref, w_ref, cos_ref, sin_ref, cn_ref, cb_ref, kr_out_ref):
    y = _rmsnorm(ckv_ref[...], w_ref[...])
    cn_ref[...] = y
    cb_ref[...] = y.astype(BF16)
    kr_out_ref[...] = _rope_lanes(kr_ref[...], cos_ref[...], sin_ref[...], QK_ROPE // 4).astype(BF16)


def _kv_post(proj, ckv_blk, kr_blk, kv_norm, cos, sin, rows_p, seq_s):
    t = proj.shape[0]
    tm = _tile(math.gcd(rows_p, seq_s), POST_ROW_TILE)
    p_tiles, s_tiles = rows_p // tm, seq_s // tm

    def pos_tile(i):
        return jnp.where(i >= p_tiles, jnp.maximum(i - p_tiles, 0) % s_tiles, s_tiles)

    vmem = 2 * (tm * KV_LORA * 10 + tm * LANE * 14) + 4 * tm * KV_LORA * 4 + 4 * MIB
    return pl.pallas_call(
        _kv_post_kernel,
        grid=(t // tm,),
        in_specs=[
            pl.BlockSpec((tm, KV_LORA), lambda i: (i, ckv_blk)),
            pl.BlockSpec((tm, LANE), lambda i: (i, kr_blk)),
            pl.BlockSpec((1, KV_LORA), lambda i: (0, 0)),
            pl.BlockSpec((tm, LANE), lambda i: (pos_tile(i), 0)),
            pl.BlockSpec((tm, LANE), lambda i: (pos_tile(i), 0)),
        ],
        out_specs=[
            pl.BlockSpec((tm, KV_LORA), lambda i: (i, 0)),
            pl.BlockSpec((tm, KV_LORA), lambda i: (i, 0)),
            pl.BlockSpec((tm, LANE), lambda i: (i, 0)),
        ],
        out_shape=[
            jax.ShapeDtypeStruct((t, KV_LORA), F32),
            jax.ShapeDtypeStruct((t, KV_LORA), BF16),
            jax.ShapeDtypeStruct((t, LANE), BF16),
        ],
        compiler_params=_params(("parallel",), vmem),
        name="odd_kv_norm_rope",
    )(proj, proj, kv_norm.reshape(1, KV_LORA), cos, sin)


def _conv_kernel(u_ref, gb_ref, gc_ref, up_ref, gcp_ref, un_ref, gcn_ref, w_ref, o_ref, *,
                 tr, rows_p, seq_p, seq_s):
    i = pl.program_id(0)
    start = i * tr
    local = jnp.where(start < rows_p, start % seq_p, jnp.maximum(start - rows_p, 0) % seq_s)
    seq = jnp.where(start < rows_p, seq_p, seq_s)
    has_prev = local > 0
    has_next = local + tr < seq
    z = gc_ref[...] * u_ref[...]
    zp = (gcp_ref[...] * up_ref[...])[SUBLANE - 1:SUBLANE, :]
    zn = (gcn_ref[...] * un_ref[...])[0:1, :]
    zp = jnp.where(has_prev, zp, 0.0)
    zn = jnp.where(has_next, zn, 0.0)
    row = lax.broadcasted_iota(I32, z.shape, 0)
    z_prev = jnp.where(row == 0, zp, pltpu.roll(z, 1, 0))
    z_next = jnp.where(row == tr - 1, zn, pltpu.roll(z, tr - 1, 0))
    w = w_ref[...]
    conv = z_prev * w[0:1, :] + z * w[1:2, :] + z_next * w[2:3, :]
    o_ref[...] = (gb_ref[...] * conv).astype(o_ref.dtype)


def _gated_conv(proj, u_blk, gb_blk, gc_blk, conv_w, rows_p, seq_p, seq_s):
    t = proj.shape[0]
    tr = _tile(math.gcd(seq_p, seq_s), POST_ROW_TILE)
    tc = _tile(D_CH, COL_TILE, LANE)
    halo = tr // SUBLANE
    last = t // SUBLANE - 1
    cpb = D_CH // tc

    def main(blk):
        return pl.BlockSpec((tr, tc), lambda i, j: (i, blk * cpb + j))

    def prev(blk):
        return pl.BlockSpec((SUBLANE, tc), lambda i, j: (jnp.maximum(i * halo - 1, 0), blk * cpb + j))

    def nxt(blk):
        return pl.BlockSpec((SUBLANE, tc), lambda i, j: (jnp.minimum((i + 1) * halo, last), blk * cpb + j))

    w8 = jnp.pad(conv_w, ((0, SUBLANE - CONV_W), (0, 0)))
    vmem = 2 * (3 * tr * tc * 4 + 4 * SUBLANE * tc * 4 + tr * tc * 2) + 8 * tr * tc * 4 + 4 * MIB
    return pl.pallas_call(
        functools.partial(_conv_kernel, tr=tr, rows_p=rows_p, seq_p=seq_p, seq_s=seq_s),
        grid=(t // tr, cpb),
        in_specs=[main(u_blk), main(gb_blk), main(gc_blk), prev(u_blk), prev(gc_blk), nxt(u_blk), nxt(gc_blk),
                  pl.BlockSpec((SUBLANE, tc), lambda i, j: (0, j))],
        out_specs=pl.BlockSpec((tr, tc), lambda i, j: (i, j)),
        out_shape=jax.ShapeDtypeStruct((t, D_CH), BF16),
        compiler_params=_params(("parallel", "parallel"), vmem),
        name="odd_gated_conv",
    )(proj, proj, proj, proj, proj, proj, proj, w8)


def _router_kernel(x_ref, nw_ref, sc_ref, sh_ref, rw_ref, rb_ref,
                   h_ref, idx_ref, wt_ref, rank_ref, cnt_ref, carry_ref):
    @pl.when(pl.program_id(0) == 0)
    def _():
        carry_ref[...] = jnp.zeros_like(carry_ref)

    h = _rmsnorm(x_ref[...], nw_ref[...]) * (1.0 + sc_ref[...]) + sh_ref[...]
    for q in range(h.shape[1] // (2 * LANE)):
        k, lanes = _pair_position(q, h_ref.shape[1])
        lo = h[:, (2 * q) * LANE:(2 * q + 1) * LANE]
        hi = h[:, (2 * q + 1) * LANE:(2 * q + 2) * LANE]
        h_ref[pl.ds(k, h.shape[0], stride=SLAB_ROWS), lanes] = _pack_pair(lo, hi)
    logits = lax.dot_general(rw_ref[...], h, (((1,), (1,)), ((), ())),
                             precision=lax.Precision.HIGHEST, preferred_element_type=F32) + rb_ref[...]
    n_e, tm = logits.shape
    eidx = lax.broadcasted_iota(I32, logits.shape, 0).astype(F32)
    work = logits
    vals, hots = [], []
    for k in range(TOP_K):
        m = jnp.max(work, axis=0, keepdims=True)
        sel = jnp.min(jnp.where(work == m, eidx, float(n_e)), axis=0, keepdims=True)
        hot = eidx == sel
        idx_ref[k:k + 1, :] = sel.astype(I32)
        vals.append(m)
        hots.append(hot)
        work = jnp.where(hot, -jnp.inf, work)
    exps = [jnp.exp(v - vals[0]) for v in vals]
    den = exps[0]
    for e in exps[1:]:
        den = den + e
    for k in range(TOP_K):
        wt_ref[k:k + 1, :] = exps[k] / den
    chosen = hots[0]
    for hot in hots[1:]:
        chosen = jnp.logical_or(chosen, hot)
    chosen = jnp.where(chosen, 1.0, 0.0)
    r = lax.broadcasted_iota(I32, (tm, tm), 0)
    c = lax.broadcasted_iota(I32, (tm, tm), 1)
    upper = jnp.where(r < c, 1.0, 0.0).astype(BF16)
    before = jnp.dot(chosen.astype(BF16), upper, preferred_element_type=F32) + carry_ref[:, 0:1]
    for k in range(TOP_K):
        rk = jnp.sum(jnp.where(hots[k], before, 0.0), axis=0, keepdims=True)
        rank_ref[k:k + 1, :] = rk.astype(I32)
    carry_ref[...] = carry_ref[...] + jnp.sum(chosen, axis=1, keepdims=True)
    cnt_ref[...] = carry_ref[...].astype(I32)


def _router(x, nw, mod, layer, router_w, router_b, rows_p, seq_s):
    t, d = x.shape
    n_e = router_w.shape[1]
    tm = _tile(math.gcd(rows_p, seq_s), ROW_TILE, LANE)
    width = _packed_width(d)

    def grp(i):
        return _group_of_row(i * tm, rows_p, seq_s)

    vmem = 2 * (tm * d * 6 + n_e * d * 4) + 4 * tm * d * 4 + tm * tm * 4 + 4 * MIB
    return pl.pallas_call(
        _router_kernel,
        grid=(t // tm,),
        in_specs=[
            pl.BlockSpec((tm, d), lambda i: (i, 0)),
            pl.BlockSpec((1, d), lambda i: (0, 0)),
            pl.BlockSpec((None, None, None, 1, d), lambda i: (layer, 4, grp(i), 0, 0)),
            pl.BlockSpec((None, None, None, 1, d), lambda i: (layer, 3, grp(i), 0, 0)),
            pl.BlockSpec((n_e, d), lambda i: (0, 0)),
            pl.BlockSpec((n_e, 1), lambda i: (0, 0)),
        ],
        out_specs=[
            pl.BlockSpec((tm * SLAB_ROWS, width), lambda i: (i, 0)),
            pl.BlockSpec((TOP_K, tm), lambda i: (0, i)),
            pl.BlockSpec((TOP_K, tm), lambda i: (0, i)),
            pl.BlockSpec((TOP_K, tm), lambda i: (0, i)),
            pl.BlockSpec((n_e, LANE), lambda i: (0, 0)),
        ],
        out_shape=[
            jax.ShapeDtypeStruct((t * SLAB_ROWS, width), jnp.uint32),
            jax.ShapeDtypeStruct((TOP_K, t), I32),
            jax.ShapeDtypeStruct((TOP_K, t), F32),
            jax.ShapeDtypeStruct((TOP_K, t), I32),
            jax.ShapeDtypeStruct((n_e, LANE), I32),
        ],
        scratch_shapes=[pltpu.VMEM((n_e, LANE), F32)],
        compiler_params=_params(("arbitrary",), vmem),
        name="moe_router",
    )(x, nw.reshape(1, d), mod, mod, router_w.T, router_b.reshape(n_e, 1))


def _slab_copy(src_hbm, row, dst, r, sem):
    return pltpu.make_async_copy(src_hbm.at[pl.ds(pl.multiple_of(row * SLAB_ROWS, SLAB_ROWS), SLAB_ROWS), :],
                                 dst.at[pl.ds(pl.multiple_of(r * SLAB_ROWS, SLAB_ROWS), SLAB_ROWS), :], sem)


def _gather_kernel(nv_ref, tok_ref, tok_next_ref, h_hbm, o_ref, buf0, buf1, sem):
    i = pl.program_id(0)
    nv = nv_ref[0]
    rows = o_ref.shape[0]
    bufs = (buf0, buf1)
    width = buf0.shape[1]

    def start_tile(tok, slot):
        def body(r2, carry):
            for u in range(2):
                r = 2 * r2 + u
                _slab_copy(h_hbm, tok[0, r], bufs[slot], r, sem.at[slot]).start(priority=u)
            return carry
        lax.fori_loop(0, rows // 2, body, 0, unroll=4)

    @pl.when(i == 0)
    def _():
        start_tile(tok_ref, 0)

    for par in (0, 1):
        @pl.when(jnp.logical_and(i % 2 == par, i + 1 < nv))
        def _():
            start_tile(tok_next_ref, 1 - par)

        @pl.when(jnp.logical_and(i % 2 == par, i < nv))
        def _():
            def body(r, carry):
                _slab_copy(h_hbm, 0, bufs[par], r, sem.at[par]).wait()
                return carry
            lax.fori_loop(0, rows, body, 0, unroll=8)
            for q in range(o_ref.shape[1] // (2 * LANE)):
                k, lanes = _pair_position(q, width)
                lo, hi = _unpack_pair(bufs[par][pl.ds(k, rows, stride=SLAB_ROWS), lanes])
                o_ref[:, (2 * q) * LANE:(2 * q + 1) * LANE] = lo.astype(o_ref.dtype)
                o_ref[:, (2 * q + 1) * LANE:(2 * q + 2) * LANE] = hi.astype(o_ref.dtype)

    @pl.when(i >= nv)
    def _():
        o_ref[...] = jnp.zeros_like(o_ref)


def _gather_rows(h_packed, d, token_of, n_valid, tg):
    nt = token_of.shape[0]
    width = _packed_width(d)
    assert tg % 2 == 0
    vmem = 2 * tg * d * 2 + 2 * tg * d * 2 + 2 * tg * d * 4 + 4 * MIB
    grid_spec = pltpu.PrefetchScalarGridSpec(
        num_scalar_prefetch=1,
        grid=(nt,),
        in_specs=[
            pl.BlockSpec((None, 1, tg), lambda i, nv: (jnp.minimum(i, nv[0] - 1), 0, 0), memory_space=pltpu.SMEM),
            pl.BlockSpec((None, 1, tg), lambda i, nv: (jnp.minimum(i + 1, nv[0] - 1), 0, 0), memory_space=pltpu.SMEM),
            pl.BlockSpec(memory_space=pl.ANY),
        ],
        out_specs=pl.BlockSpec((tg, d), lambda i, nv: (i, 0)),
        scratch_shapes=[pltpu.VMEM((tg * SLAB_ROWS, width), jnp.uint32),
                        pltpu.VMEM((tg * SLAB_ROWS, width), jnp.uint32), pltpu.SemaphoreType.DMA((2,))],
    )
    return pl.pallas_call(
        _gather_kernel,
        grid_spec=grid_spec,
        out_shape=jax.ShapeDtypeStruct((nt * tg, d), BF16),
        compiler_params=_params(("arbitrary",), vmem),
        name="moe_dispatch_gather",
    )(n_valid, token_of, token_of, h_packed)


def _gate_up_kernel(te_ref, nv_ref, x_ref, wg_ref, wu_ref, bg_ref, bu_ref, o_ref, wg_bf, wu_bf):
    r = pl.program_id(1)
    e = te_ref[r]
    e_prev = te_ref[jnp.maximum(r - 1, 0)]

    @pl.when(jnp.logical_or(r == 0, e != e_prev))
    def _():
        wg_bf[...] = wg_ref[...].astype(BF16)
        wu_bf[...] = wu_ref[...].astype(BF16)

    @pl.when(r < nv_ref[0])
    def _():
        x = x_ref[...]
        g = jnp.dot(x, wg_bf[...], preferred_element_type=F32) + bg_ref[...]
        u = jnp.dot(x, wu_bf[...], preferred_element_type=F32) + bu_ref[...]
        g = jnp.minimum(g, SWIGLU_LIMIT)
        u = jnp.clip(u, -SWIGLU_LIMIT, SWIGLU_LIMIT)
        o_ref[...] = ((u + 1.0) * g * jax.nn.sigmoid(SWIGLU_ALPHA * g)).astype(o_ref.dtype)

    @pl.when(r >= nv_ref[0])
    def _():
        o_ref[...] = jnp.zeros_like(o_ref)


def _gate_up(xs, tile_expert, n_valid, layer, w_gate, w_up, b_gate, b_up, tm):
    p, d = xs.shape
    depth, n_e, _, ff = w_gate.shape
    nt = p // tm
    nc = _tile(ff, GATE_UP_COL_TILE, LANE)

    def row(c, r, te, nv):
        return jnp.minimum(r, nv[0] - 1)

    vmem = 2 * (tm * d * 2 + 2 * d * nc * 4 + tm * nc * 2) + 2 * d * nc * 2 + 6 * tm * nc * 4 + 4 * MIB
    grid_spec = pltpu.PrefetchScalarGridSpec(
        num_scalar_prefetch=2,
        grid=(ff // nc, nt),
        in_specs=[
            pl.BlockSpec((tm, d), lambda c, r, te, nv: (row(c, r, te, nv), 0)),
            pl.BlockSpec((None, None, d, nc), lambda c, r, te, nv: (layer, te[r], 0, c)),
            pl.BlockSpec((None, None, d, nc), lambda c, r, te, nv: (layer, te[r], 0, c)),
            pl.BlockSpec((None, None, 1, nc), lambda c, r, te, nv: (layer, te[r], 0, c)),
            pl.BlockSpec((None, None, 1, nc), lambda c, r, te, nv: (layer, te[r], 0, c)),
        ],
        out_specs=pl.BlockSpec((tm, nc), lambda c, r, te, nv: (r, c)),
        scratch_shapes=[pltpu.VMEM((d, nc), BF16), pltpu.VMEM((d, nc), BF16)],
    )
    return pl.pallas_call(
        _gate_up_kernel,
        grid_spec=grid_spec,
        out_shape=jax.ShapeDtypeStruct((p, ff), BF16),
        compiler_params=_params(("arbitrary", "arbitrary"), vmem),
        name="moe_gate_up",
    )(tile_expert, n_valid, xs, w_gate, w_up, b_gate.reshape(depth, n_e, 1, ff), b_up.reshape(depth, n_e, 1, ff))


def _down_kernel(te_ref, nv_ref, a_ref, wd_ref, bd_ref, o_ref, wd_bf):
    r = pl.program_id(0)
    e = te_ref[r]
    e_prev = te_ref[jnp.maximum(r - 1, 0)]
    rows = a_ref.shape[0]

    @pl.when(jnp.logical_or(r == 0, e != e_prev))
    def _():
        wd_bf[...] = wd_ref[...].astype(BF16)

    @pl.when(r < nv_ref[0])
    def _():
        a = a_ref[...]
        for q in range(wd_bf.shape[1] // (2 * LANE)):
            sl = slice(2 * q * LANE, (2 * q + 2) * LANE)
            y = jnp.dot(a, wd_bf[:, sl], preferred_element_type=F32) + bd_ref[:, sl]
            k, lanes = _pair_position(q, o_ref.shape[1])
            o_ref[pl.ds(k, rows, stride=SLAB_ROWS), lanes] = _pack_pair(y[:, :LANE], y[:, LANE:])

    @pl.when(r >= nv_ref[0])
    def _():
        o_ref[...] = jnp.zeros_like(o_ref)


def _down(act, tile_expert, n_valid, layer, w_down, b_down, tm):
    p, ff = act.shape
    depth, n_e, _, d = w_down.shape
    nt = p // tm
    width = _packed_width(d)
    vmem = 2 * (tm * ff * 2 + tm * d * 2 + ff * d * 4) + ff * d * 2 + 4 * tm * 2 * LANE * 4 + 4 * MIB
    grid_spec = pltpu.PrefetchScalarGridSpec(
        num_scalar_prefetch=2,
        grid=(nt,),
        in_specs=[
            pl.BlockSpec((tm, ff), lambda r, te, nv: (jnp.minimum(r, nv[0] - 1), 0)),
            pl.BlockSpec((None, None, ff, d), lambda r, te, nv: (layer, te[r], 0, 0)),
            pl.BlockSpec((None, None, 1, d), lambda r, te, nv: (layer, te[r], 0, 0)),
        ],
        out_specs=pl.BlockSpec((tm * SLAB_ROWS, width), lambda r, te, nv: (r, 0)),
        scratch_shapes=[pltpu.VMEM((ff, d), BF16)],
    )
    return pl.pallas_call(
        _down_kernel,
        grid_spec=grid_spec,
        out_shape=jax.ShapeDtypeStruct((p * SLAB_ROWS, width), jnp.uint32),
        compiler_params=_params(("arbitrary",), vmem),
        name="moe_down",
    )(tile_expert, n_valid, act, w_down, b_down.reshape(depth, n_e, 1, d))


def _combine_kernel(pos_ref, pos_next_ref, x_ref, wt_ref, g_ref, fw_ref, y_hbm, o_ref, buf0, buf1, sem, *,
                    final_norm):
    i = pl.program_id(0)
    n = pl.num_programs(0)
    rows = x_ref.shape[0]
    bufs = (buf0, buf1)
    width = buf0.shape[1]

    def start_tile(pos, slot):
        def body(r, carry):
            for k in range(TOP_K):
                _slab_copy(y_hbm, pos[k, r], bufs[slot], k * rows + r, sem.at[slot]).start(priority=k % 2)
            return carry
        lax.fori_loop(0, rows, body, 0, unroll=4)

    @pl.when(i == 0)
    def _():
        start_tile(pos_ref, 0)

    for par in (0, 1):
        @pl.when(jnp.logical_and(i % 2 == par, i + 1 < n))
        def _():
            start_tile(pos_next_ref, 1 - par)

        @pl.when(i % 2 == par)
        def _():
            def wait_body(r, carry):
                for k in range(TOP_K):
                    _slab_copy(y_hbm, 0, bufs[par], k * rows + r, sem.at[par]).wait()
                return carry
            lax.fori_loop(0, rows, wait_body, 0, unroll=4)
            wt = wt_ref[...]
            wb = [jnp.broadcast_to(wt[:, k:k + 1], (rows, LANE)) for k in range(TOP_K)]
            for c in range(x_ref.shape[1] // (2 * LANE)):
                row, lanes = _pair_position(c, width)
                acc_lo = acc_hi = None
                for k in range(TOP_K):
                    lo, hi = _unpack_pair(
                        bufs[par][pl.ds(k * rows * SLAB_ROWS + row, rows, stride=SLAB_ROWS), lanes])
                    acc_lo = wb[k] * lo if acc_lo is None else acc_lo + wb[k] * lo
                    acc_hi = wb[k] * hi if acc_hi is None else acc_hi + wb[k] * hi
                s_lo = slice((2 * c) * LANE, (2 * c + 1) * LANE)
                s_hi = slice((2 * c + 1) * LANE, (2 * c + 2) * LANE)
                o_ref[:, s_lo] = x_ref[:, s_lo] + g_ref[:, s_lo] * acc_lo
                o_ref[:, s_hi] = x_ref[:, s_hi] + g_ref[:, s_hi] * acc_hi

    if final_norm:
        o_ref[...] = _rmsnorm(o_ref[...], fw_ref[...])


def _combine(x, y_pairs, pos, wts, mod, layer, final_w, rows_p, seq_s, *, final_norm):
    t, d = x.shape
    tc = _tile(math.gcd(rows_p, seq_s), COMBINE_ROW_TILE)
    nt = t // tc
    pos3 = pos.reshape(TOP_K, nt, tc).transpose(1, 0, 2)

    def grp(i):
        return _group_of_row(i * tc, rows_p, seq_s)

    width = _packed_width(d)
    vmem = 2 * TOP_K * tc * d * 2 + 2 * (2 * tc * d * 4 + tc * LANE * 4) + 3 * tc * d * 4 + 4 * MIB
    return pl.pallas_call(
        functools.partial(_combine_kernel, final_norm=final_norm),
        grid=(nt,),
        in_specs=[
            pl.BlockSpec((None, TOP_K, tc), lambda i: (i, 0, 0), memory_space=pltpu.SMEM),
            pl.BlockSpec((None, TOP_K, tc), lambda i: (jnp.minimum(i + 1, nt - 1), 0, 0), memory_space=pltpu.SMEM),
            pl.BlockSpec((tc, d), lambda i: (i, 0)),
            pl.BlockSpec((tc, TOP_K), lambda i: (i, 0)),
            pl.BlockSpec((None, None, None, 1, d), lambda i: (layer, 5, grp(i), 0, 0)),
            pl.BlockSpec((1, d), lambda i: (0, 0)),
            pl.BlockSpec(memory_space=pl.ANY),
        ],
        out_specs=pl.BlockSpec((tc, d), lambda i: (i, 0)),
        out_shape=jax.ShapeDtypeStruct((t, d), F32),
        scratch_shapes=[pltpu.VMEM((TOP_K * tc * SLAB_ROWS, width), jnp.uint32),
                        pltpu.VMEM((TOP_K * tc * SLAB_ROWS, width), jnp.uint32), pltpu.SemaphoreType.DMA((2,))],
        compiler_params=_params(("arbitrary",), vmem),
        name="moe_combine",
    )(pos3, pos3, x, wts.T, mod, final_w.reshape(1, d), y_pairs)


def _moe_layer(x, nw, mod, layer, router_w, router_b, w_gate, b_gate, w_up, b_up, w_down, b_down,
               final_w, rows_p, seq_s, *, final_norm):
    t, d = x.shape
    n_e = router_w.shape[1]
    tm = EXPERT_ROW_TILE
    tg = _tile(tm, GATHER_ROW_TILE)
    h, idx, wts, rank, cnt = _router(x, nw, mod, layer, router_w, router_b, rows_p, seq_s)
    counts = cnt[:, 0]
    tiles_e = (counts + tm - 1) // tm
    e_ids = jnp.arange(n_e, dtype=I32)
    tile_end = jnp.sum(jnp.where(e_ids[None, :] <= e_ids[:, None], tiles_e[None, :], 0), axis=1)
    offs = (tile_end - tiles_e) * tm
    pos = jnp.sum(jnp.where(idx[:, :, None] == e_ids[None, None, :], offs[None, None, :], 0), axis=-1) + rank
    n_tiles = (t * TOP_K + tm - 1) // tm + n_e
    p_rows = n_tiles * tm
    n_valid_tiles = tile_end[-1]
    tok = jnp.broadcast_to(jnp.arange(t, dtype=I32)[None, :], (TOP_K, t))
    token_of = jnp.zeros((p_rows,), I32).at[pos.reshape(-1)].set(tok.reshape(-1))
    tile_ids = jnp.minimum(jnp.arange(n_tiles, dtype=I32), n_valid_tiles - 1)
    tile_expert = jnp.sum((tile_end[None, :] <= tile_ids[:, None]).astype(I32), axis=1)
    nv_m = n_valid_tiles.reshape(1).astype(I32)
    nv_g = (n_valid_tiles * (tm // tg)).reshape(1).astype(I32)
    xs = _gather_rows(h, d, token_of.reshape(p_rows // tg, 1, tg), nv_g, tg)
    act = _gate_up(xs, tile_expert, nv_m, layer, w_gate, w_up, b_gate, b_up, tm)
    y_pairs = _down(act, tile_expert, nv_m, layer, w_down, b_down, tm)
    return _combine(x, y_pairs, pos, wts, mod, layer, final_w, rows_p, seq_s, final_norm=final_norm)


def _even_mixer(x, mod, layer, nw, w_in, q_norm, k_norm, w_out, cache_k, cache_v, dims):
    rows_p, seq_p, n_bp, seq_s, n_bs = dims
    t = x.shape[0]
    qw_, kvw = A_HEADS * HEAD_DIM, A_KV_HEADS * HEAD_DIM
    past = cache_k.shape[1]
    nk_s = past + seq_s
    proj = _norm_matmul(x, 0, nw, w_in.astype(BF16), rows_p, seq_s, mod=mod, layer=layer,
                        m_shift=0, m_scale=1, name="even_in_proj")
    tm_post = _tile(math.gcd(rows_p, seq_s), POST_ROW_TILE)
    cos, sin = _rope_tables(seq_s, HEAD_DIM, tm_post)
    qh, kh, vh, kst = _qk_post(proj, q_norm, k_norm, cos, sin, rows_p, seq_s)
    scale = HEAD_DIM ** -0.5
    group = A_HEADS // A_KV_HEADS
    attn = _attention(qh, kh, vh, n_batch=n_bp, sq=seq_p, nk=seq_p, q_row0=0, k_blk0=0, k_col0=0, v_col0=0,
                      n_kv=A_KV_HEADS, group=group, scale=scale, out_cols=qw_, prev=jnp.zeros((t, qw_), BF16),
                      name="even_attention_ctx")
    k_lat = kh[rows_p:].reshape(n_bs, seq_s, kvw)
    v_lat = vh[rows_p:].reshape(n_bs, seq_s, kvw)
    k_all = jnp.concatenate([cache_k.reshape(n_bs, past, kvw).astype(BF16), k_lat], axis=1).reshape(n_bs * nk_s, kvw)
    v_all = jnp.concatenate([cache_v.reshape(n_bs, past, kvw).astype(BF16), v_lat], axis=1).reshape(n_bs * nk_s, kvw)
    attn = _attention(qh, k_all, v_all, n_batch=n_bs, sq=seq_s, nk=nk_s, q_row0=rows_p, k_blk0=0, k_col0=0,
                      v_col0=0, n_kv=A_KV_HEADS, group=group, scale=scale, out_cols=qw_, prev=attn,
                      name="even_attention_latent")
    ab = _dft_width(proj, (qw_ + 2 * kvw) // B_GROUP_W, rows_p, seq_s)
    four = _dft_seq(ab, n_bp, seq_p, 0, jnp.zeros((t, B_GROUPS * B_GROUP_W), BF16), "fourier_seq_dft_ctx")
    four = _dft_seq(ab, n_bs, seq_s, rows_p, four, "fourier_seq_dft_latent")
    w_out_bf = w_out.astype(BF16)
    x_new = _out_proj(attn, four, w_out_bf[:qw_], w_out_bf[qw_:], x, mod, layer, 2, rows_p, seq_s,
                      name="even_out_proj")
    k_state = kst[:rows_p].reshape(n_bp, seq_p, A_KV_HEADS, HEAD_DIM)
    v_state = proj[:rows_p, qw_ + kvw:qw_ + 2 * kvw].reshape(n_bp, seq_p, A_KV_HEADS, HEAD_DIM)
    return x_new, k_state, v_state


def _odd_mixer(x, mod, layer, nw, w_in, q_a_norm, kv_a_norm, w_uq, w_ukv, conv_w, w_out,
               cache_ckv, cache_kr, dims):
    rows_p, seq_p, n_bp, seq_s, n_bs = dims
    d = x.shape[1]
    past = cache_ckv.shape[1]
    nk_s = past + seq_s
    o1, o2, o3 = Q_LORA, Q_LORA + KV_LORA, Q_LORA + KV_LORA + QK_ROPE
    o4, o5 = o3 + D_CH, o3 + 2 * D_CH
    pad_w = LANE - QK_ROPE
    w_in_r = jnp.concatenate([w_in[:, o3:o4], w_in[:, o4:o5], w_in[:, o5:], w_in[:, o1:o2], w_in[:, :o1],
                              w_in[:, o2:o3], jnp.zeros((d, pad_w), w_in.dtype)], axis=1).astype(BF16)
    c_ckv = 3 * D_CH
    c_cq = c_ckv + KV_LORA
    c_kr = c_cq + Q_LORA
    assert c_ckv % KV_LORA == 0 and c_cq % Q_LORA == 0 and c_kr % LANE == 0
    proj = _norm_matmul(x, 0, nw, w_in_r, rows_p, seq_s, mod=mod, layer=layer, m_shift=0, m_scale=1,
                        name="odd_in_proj")
    hq = QK_NOPE + QK_ROPE
    wq = w_uq.reshape(Q_LORA, C_HEADS, hq)
    wq_rope = jnp.pad(wq[:, :, QK_NOPE:], ((0, 0), (0, 0), (0, pad_w)))
    wq_r = jnp.concatenate([wq[:, :, :QK_NOPE].reshape(Q_LORA, C_HEADS * QK_NOPE),
                            wq_rope.reshape(Q_LORA, C_HEADS * LANE)], axis=1).astype(BF16)
    tm = _tile(math.gcd(rows_p, seq_s), ROW_TILE)
    tn = _tile(wq_r.shape[1], IN_PROJ_COL_TILE, LANE)
    assert (C_HEADS * QK_NOPE) % tn == 0
    cos_q, sin_q = _rope_tables(seq_s, QK_ROPE, tm)
    q = _norm_matmul(proj, c_cq // Q_LORA, q_a_norm, wq_r, rows_p, seq_s,
                     rope=(cos_q, sin_q, QK_ROPE // 4, (C_HEADS * QK_NOPE) // tn), out_dtype=BF16,
                     name="odd_q_up_proj")
    tm_post = _tile(math.gcd(rows_p, seq_s), POST_ROW_TILE)
    cos_k, sin_k = _rope_tables(seq_s, QK_ROPE, tm_post)
    ckvn, ckvn_bf, kr = _kv_post(proj, c_ckv // KV_LORA, c_kr // LANE, kv_a_norm, cos_k, sin_k, rows_p, seq_s)
    hkv = QK_NOPE + V_HEAD
    wkv = w_ukv.reshape(KV_LORA, C_HEADS, hkv)
    wkv_r = jnp.concatenate([wkv[:, :, :QK_NOPE].reshape(KV_LORA, C_HEADS * QK_NOPE),
                             wkv[:, :, QK_NOPE:].reshape(KV_LORA, C_HEADS * V_HEAD)], axis=1).astype(BF16)
    n_kv_cols = wkv_r.shape[1]
    ckv_all = jnp.concatenate([cache_ckv.astype(BF16), ckvn_bf[rows_p:].reshape(n_bs, seq_s, KV_LORA)],
                              axis=1).reshape(n_bs * nk_s, KV_LORA)
    kr_cache = jnp.pad(cache_kr, ((0, 0), (0, 0), (0, pad_w))).astype(BF16)
    kr_all = jnp.concatenate([kr_cache, kr[rows_p:].reshape(n_bs, seq_s, LANE)], axis=1).reshape(n_bs * nk_s, LANE)

    def kv_up(lat, name):
        m = lat.shape[0]
        tmm = _tile(m, ROW_TILE)
        tnn = _tile(n_kv_cols, 2 * COL_TILE, LANE)
        return _matmul(lat, wkv_r, grid=(1, m // tmm, n_kv_cols // tnn, 1),
                       a_spec=pl.BlockSpec((tmm, KV_LORA), lambda b, i, j, kk: (i, 0)),
                       b_spec=pl.BlockSpec((KV_LORA, tnn), lambda b, i, j, kk: (0, j)),
                       o_spec=pl.BlockSpec((tmm, tnn), lambda b, i, j, kk: (i, j)),
                       out_shape=jax.ShapeDtypeStruct((m, n_kv_cols), BF16),
                       tile=(tmm, tnn, KV_LORA), name=name)

    kv_p = kv_up(ckvn_bf[:rows_p], "odd_kv_up_proj_ctx")
    kv_s = kv_up(ckv_all, "odd_kv_up_proj_latent")
    scale = hq ** -0.5
    out_cols = C_HEADS * V_HEAD
    attn = _attention(q, kv_p, kv_p, n_batch=n_bp, sq=seq_p, nk=seq_p, q_row0=0, k_blk0=0, k_col0=0,
                      v_col0=C_HEADS, n_kv=C_HEADS, group=1, scale=scale, out_cols=out_cols,
                      q2=q, q2_col0=C_HEADS, k2=kr, k2_blk0=0, prev=jnp.zeros((x.shape[0], out_cols), BF16),
                      name="odd_attention_ctx")
    attn = _attention(q, kv_s, kv_s, n_batch=n_bs, sq=seq_s, nk=nk_s, q_row0=rows_p, k_blk0=0, k_col0=0,
                      v_col0=C_HEADS, n_kv=C_HEADS, group=1, scale=scale, out_cols=out_cols,
                      q2=q, q2_col0=C_HEADS, k2=kr_all, k2_blk0=0, prev=attn, name="odd_attention_latent")
    conv = _gated_conv(proj, 0, 1, 2, conv_w, rows_p, seq_p, seq_s)
    w_out_bf = w_out.astype(BF16)
    x_new = _out_proj(attn, conv, w_out_bf[:out_cols], w_out_bf[out_cols:], x, mod, layer, 2, rows_p, seq_s,
                      name="odd_out_proj")
    ckv_state = ckvn[:rows_p].reshape(n_bp, seq_p, KV_LORA)
    kr_state = proj[:rows_p, c_kr:c_kr + QK_ROPE].reshape(n_bp, seq_p, QK_ROPE)
    return x_new, ckv_state, kr_state


def kernel(x_prompt, x_sample, cache_attn_k, cache_attn_v, cache_mla_ckv, cache_mla_krope, c, c_ctx,
           norm1_w, norm2_w, w_ada, b_ada,
           even_w_in, even_q_norm, even_k_norm, even_w_out,
           odd_w_in, odd_q_a_norm, odd_kv_a_norm, odd_w_uq, odd_w_ukv, odd_conv_w, odd_w_out,
           router_w, router_b, w_gate, b_gate, w_up, b_up, w_down, b_down, final_norm_w):
    n_bp, seq_p, d = x_prompt.shape
    n_bs, seq_s, _ = x_sample.shape
    depth = w_ada.shape[0]
    rows_p = n_bp * seq_p
    dims = (rows_p, seq_p, n_bp, seq_s, n_bs)
    n_groups = 1 + n_bs
    g8 = -(-n_groups // SUBLANE) * SUBLANE
    cond = jnp.concatenate([c_ctx[None, :], c, jnp.zeros((g8 - n_groups, d), F32)], axis=0)
    ada = _ada_modulation(cond, w_ada, b_ada)
    mod = ada[:, :n_groups].reshape(depth, n_groups, N_MOD, d).transpose(0, 2, 1, 3)[:, :, :, None, :]
    x = jnp.concatenate([x_prompt.reshape(rows_p, d), x_sample.reshape(n_bs * seq_s, d)], axis=0)
    st_k, st_v, st_ckv, st_kr = [], [], [], []
    for l in range(depth):
        j = l // 2
        if l % 2 == 0:
            x, ks, vs = _even_mixer(x, mod, l, norm1_w[l], even_w_in[j], even_q_norm[j], even_k_norm[j],
                                    even_w_out[j], cache_attn_k[:, j], cache_attn_v[:, j], dims)
            st_k.append(ks)
            st_v.append(vs)
        else:
            x, cs, rs = _odd_mixer(x, mod, l, norm1_w[l], odd_w_in[j], odd_q_a_norm[j], odd_kv_a_norm[j],
                                   odd_w_uq[j], odd_w_ukv[j], odd_conv_w[j], odd_w_out[j],
                                   cache_mla_ckv[:, j], cache_mla_krope[:, j], dims)
            st_ckv.append(cs)
            st_kr.append(rs)
        x = _moe_layer(x, norm2_w[l], mod, l, router_w[l], router_b[l], w_gate, b_gate, w_up, b_up,
                       w_down, b_down, final_norm_w, rows_p, seq_s, final_norm=(l == depth - 1))
    y_prompt = x[:rows_p].reshape(n_bp, seq_p, d)
    y_sample = x[rows_p:].reshape(n_bs, seq_s, d)
    return (y_prompt, y_sample, jnp.stack(st_k, axis=1), jnp.stack(st_v, axis=1),
            jnp.stack(st_ckv, axis=1), jnp.stack(st_kr, axis=1))
```

```python
import functools
import math

import jax
import jax.numpy as jnp
from jax import lax
from jax.experimental import pallas as pl
from jax.experimental.pallas import tpu as pltpu

F32 = jnp.float32
BF16 = jnp.bfloat16
I32 = jnp.int32

GRID_W = 64
HEAD_DIM = 128
ROPE_THETA = 10000.0
NORM_EPS = 1e-6
A_HEADS = 24
A_KV_HEADS = 6
B_GROUPS = 8
B_GROUP_W = 128
C_HEADS = 24
Q_LORA = 896
KV_LORA = 512
QK_NOPE = 128
QK_ROPE = 64
V_HEAD = 128
D_CH = 1024
CONV_W = 3
N_EXPERTS = 32
TOP_K = 4
EXPERT_FF = 1024
SWIGLU_LIMIT = 7.0
SWIGLU_ALPHA = 1.702
N_MOD = 6
LOG2_E = math.log2(math.e)

LANE = 128
SUBLANE = 8
D_MODEL = 4096
SLAB_ROWS = D_MODEL // (2 * LANE)
V7X_VMEM_BYTES = 64 * 2**20
MIB = 2**20

ROW_TILE = 512
COL_TILE = 512
IN_PROJ_COL_TILE = 1408
NORM_ROW_CHUNK = 64
POST_ROW_TILE = 256
ATTN_Q_TILE = 256
CTX_KV_HEADS_PER_STEP = 8
LATENT_MLA_HEADS_PER_STEP = 2
ADA_COL_TILE = 512
EXPERT_ROW_TILE = 512
GATE_UP_COL_TILE = 512
GATHER_ROW_TILE = 256
COMBINE_ROW_TILE = 128


def _tile(n, pref, align=SUBLANE):
    t = min(pref, n)
    t -= t % align
    while t > align and n % t:
        t -= align
    assert t >= align and n % t == 0, (n, pref, align)
    return t


def _params(sem, vmem_bytes):
    limit = int(min(max(vmem_bytes, 16 * MIB), V7X_VMEM_BYTES - 6 * MIB))
    return pltpu.CompilerParams(dimension_semantics=sem, vmem_limit_bytes=limit)


def _group_of_row(start, rows_p, seq_s):
    return jnp.where(start < rows_p, 0, 1 + jnp.maximum(start - rows_p, 0) // seq_s)


def _ada_kernel(c_ref, w_ref, b_ref, o_ref):
    c = c_ref[...]
    s = (c * jax.nn.sigmoid(c)).astype(BF16)
    o_ref[...] = jnp.dot(s, w_ref[...].astype(BF16), preferred_element_type=F32) + b_ref[...]


def _ada_modulation(cond, w_ada, b_ada):
    depth, d, n = w_ada.shape
    g8 = cond.shape[0]
    tn = _tile(n, ADA_COL_TILE, LANE)
    vmem = 2 * (d * tn * 4) + d * tn * 2 + 4 * MIB
    return pl.pallas_call(
        _ada_kernel,
        grid=(depth, n // tn),
        in_specs=[
            pl.BlockSpec((g8, d), lambda l, j: (0, 0)),
            pl.BlockSpec((None, d, tn), lambda l, j: (l, 0, j)),
            pl.BlockSpec((None, 1, tn), lambda l, j: (l, 0, j)),
        ],
        out_specs=pl.BlockSpec((None, g8, tn), lambda l, j: (l, 0, j)),
        out_shape=jax.ShapeDtypeStruct((depth, g8, n), F32),
        compiler_params=_params(("parallel", "parallel"), vmem),
        name="ada_modulation",
    )(cond, w_ada, b_ada.reshape(depth, 1, n))


def _rmsnorm(x, w):
    return x * lax.rsqrt(jnp.mean(x * x, axis=-1, keepdims=True) + NORM_EPS) * w


def _pack_pair(lo, hi):
    lo_b = lax.bitcast_convert_type(lo.astype(jnp.bfloat16).astype(F32), jnp.uint32)
    hi_b = lax.bitcast_convert_type(hi.astype(jnp.bfloat16).astype(F32), jnp.uint32)
    return hi_b | (lo_b >> 16)


def _packed_width(d):
    assert d == 2 * LANE * SLAB_ROWS, (d, SLAB_ROWS)
    return LANE


def _pair_position(q, width):
    assert width == LANE
    return q, slice(0, LANE)


def _unpack_pair(w):
    lo = lax.bitcast_convert_type(w << 16, F32)
    hi = lax.bitcast_convert_type(w & jnp.uint32(0xFFFF0000), F32)
    return lo, hi


def _rope_lanes(y, cos, sin, half):
    lane = lax.broadcasted_iota(I32, y.shape, 1)
    partner = jnp.where(lane % (2 * half) < half,
                        pltpu.roll(y, LANE - half, 1),
                        pltpu.roll(y, half, 1))
    return y * cos + partner * sin


def _rope_tables(n_tokens, rot_dim, ident_rows):
    rows = n_tokens // GRID_W
    row = jnp.repeat(jnp.arange(rows), GRID_W).astype(F32)
    col = jnp.tile(jnp.arange(GRID_W), rows).astype(F32)
    half = rot_dim // 2
    inv = ROPE_THETA ** (-jnp.arange(0, half, 2, dtype=F32) / half)
    ang_r = row[:, None] * inv[None, :]
    ang_c = col[:, None] * inv[None, :]
    ang = jnp.concatenate([ang_r, ang_r, ang_c, ang_c], axis=-1)
    sign = jnp.tile(jnp.concatenate([-jnp.ones((half // 2,), F32), jnp.ones((half // 2,), F32)]), 2)
    cos = jnp.cos(ang)
    sin = jnp.sin(ang) * sign[None, :]
    pad = LANE - rot_dim
    cos = jnp.pad(cos, ((0, ident_rows), (0, pad)), constant_values=1.0)
    sin = jnp.pad(sin, ((0, ident_rows), (0, pad)), constant_values=0.0)
    return cos, sin


def _norm_matmul_kernel(*refs, modulate, rope_half):
    it = iter(refs)
    x_ref, nw_ref = next(it), next(it)
    sc_ref = sh_ref = cos_ref = sin_ref = None
    if modulate:
        sc_ref, sh_ref = next(it), next(it)
    w_ref = next(it)
    if rope_half:
        cos_ref, sin_ref = next(it), next(it)
    o_ref, h_ref = next(it), next(it)

    @pl.when(pl.program_id(1) == 0)
    def _():
        rc = _tile(x_ref.shape[0], NORM_ROW_CHUNK, 2 * SUBLANE)

        def chunk(r, carry):
            rows = pl.ds(pl.multiple_of(r * rc, rc), rc)
            y = _rmsnorm(x_ref[rows, :], nw_ref[...])
            if modulate:
                y = y * (1.0 + sc_ref[...]) + sh_ref[...]
            h_ref[rows, :] = y.astype(BF16)
            return carry
        lax.fori_loop(0, x_ref.shape[0] // rc, chunk, 0)

    acc = jnp.dot(h_ref[...], w_ref[...], preferred_element_type=F32)
    if rope_half:
        cos, sin = cos_ref[...], sin_ref[...]
        for c in range(acc.shape[1] // LANE):
            sl = slice(c * LANE, (c + 1) * LANE)
            o_ref[:, sl] = _rope_lanes(acc[:, sl], cos, sin, rope_half).astype(o_ref.dtype)
    else:
        o_ref[...] = acc.astype(o_ref.dtype)


def _norm_matmul(x, x_col, nw, w, rows_p, seq_s, *, mod=None, layer=0, m_shift=0, m_scale=0,
                 rope=None, out_dtype=F32, name):
    t = x.shape[0]
    k, n = w.shape
    tm = _tile(math.gcd(rows_p, seq_s), ROW_TILE)
    tn = _tile(n, IN_PROJ_COL_TILE, LANE)
    p_tiles = rows_p // tm
    s_tiles = seq_s // tm

    def grp(i):
        return _group_of_row(i * tm, rows_p, seq_s)

    in_specs = [pl.BlockSpec((tm, k), lambda i, j: (i, x_col)),
                pl.BlockSpec((1, k), lambda i, j: (0, 0))]
    args = [x, nw.reshape(1, k)]
    if mod is not None:
        in_specs += [pl.BlockSpec((None, None, None, 1, k), lambda i, j: (layer, m_scale, grp(i), 0, 0)),
                     pl.BlockSpec((None, None, None, 1, k), lambda i, j: (layer, m_shift, grp(i), 0, 0))]
        args += [mod, mod]
    in_specs.append(pl.BlockSpec((k, tn), lambda i, j: (0, j)))
    args.append(w)
    rope_half = 0
    if rope is not None:
        cos, sin, rope_half, first_tile = rope

        def pos_tile(i, j):
            is_pos = jnp.logical_and(i >= p_tiles, j >= first_tile)
            return jnp.where(is_pos, jnp.maximum(i - p_tiles, 0) % s_tiles, s_tiles)

        in_specs += [pl.BlockSpec((tm, LANE), lambda i, j: (pos_tile(i, j), 0)),
                     pl.BlockSpec((tm, LANE), lambda i, j: (pos_tile(i, j), 0))]
        args += [cos, sin]
    osz = jnp.dtype(out_dtype).itemsize
    vmem = 2 * (tm * k * 4 + k * tn * 2 + tm * tn * osz) + tm * k * 2 + 2 * tm * tn * 4 + 4 * MIB
    return pl.pallas_call(
        functools.partial(_norm_matmul_kernel, modulate=mod is not None, rope_half=rope_half),
        grid=(t // tm, n // tn),
        in_specs=in_specs,
        out_specs=pl.BlockSpec((tm, tn), lambda i, j: (i, j)),
        out_shape=jax.ShapeDtypeStruct((t, n), out_dtype),
        scratch_shapes=[pltpu.VMEM((tm, k), BF16)],
        compiler_params=_params(("parallel", "arbitrary"), vmem),
        name=name,
    )(*args)


def _matmul_kernel(*refs, scale, aliased):
    a_ref, b_ref = refs[0], refs[1]
    o_ref, acc_ref = refs[-2], refs[-1]
    del aliased
    kk = pl.program_id(3)

    @pl.when(kk == 0)
    def _():
        acc_ref[...] = jnp.zeros_like(acc_ref)

    acc_ref[...] += jnp.dot(a_ref[...], b_ref[...], preferred_element_type=F32)

    @pl.when(kk == pl.num_programs(3) - 1)
    def _():
        acc = acc_ref[...]
        if scale != 1.0:
            acc = acc * scale
        o_ref[...] = acc.astype(o_ref.dtype)


def _matmul(a, b, *, grid, a_spec, b_spec, o_spec, out_shape, tile, scale=1.0, prev=None, name):
    tm, tn, tk = tile
    in_specs = [a_spec, b_spec]
    args = [a, b]
    aliases = {}
    if prev is not None:
        in_specs.append(pl.BlockSpec(memory_space=pl.ANY))
        args.append(prev)
        aliases = {2: 0}
    osz = jnp.dtype(out_shape.dtype).itemsize
    vmem = 2 * (tm * tk * 2 + tk * tn * 2 + tm * tn * osz) + 2 * tm * tn * 4 + 4 * MIB
    return pl.pallas_call(
        functools.partial(_matmul_kernel, scale=scale, aliased=prev is not None),
        grid=grid,
        in_specs=in_specs,
        out_specs=o_spec,
        out_shape=out_shape,
        scratch_shapes=[pltpu.VMEM((tm, tn), F32)],
        input_output_aliases=aliases,
        compiler_params=_params(("parallel", "parallel", "parallel", "arbitrary"), vmem),
        name=name,
    )(*args)


def _out_proj_kernel(a1_ref, a2_ref, w1_ref, w2_ref, x_ref, g_ref, o_ref):
    acc = jnp.dot(a1_ref[...], w1_ref[...], preferred_element_type=F32)
    acc += jnp.dot(a2_ref[...], w2_ref[...], preferred_element_type=F32)
    o_ref[...] = x_ref[...] + g_ref[...] * acc


def _out_proj(a1, a2, w1, w2, x, mod, layer, m_gate, rows_p, seq_s, *, name):
    t, d = x.shape
    k1, k2 = a1.shape[1], a2.shape[1]
    tm = _tile(math.gcd(rows_p, seq_s), ROW_TILE)
    tn = _tile(d, 2 * COL_TILE, LANE)

    def grp(i):
        return _group_of_row(i * tm, rows_p, seq_s)

    vmem = 2 * (tm * (k1 + k2) * 2 + (k1 + k2) * tn * 2 + 2 * tm * tn * 4) + 2 * tm * tn * 4 + 4 * MIB
    return pl.pallas_call(
        _out_proj_kernel,
        grid=(t // tm, d // tn),
        in_specs=[
            pl.BlockSpec((tm, k1), lambda i, j: (i, 0)),
            pl.BlockSpec((tm, k2), lambda i, j: (i, 0)),
            pl.BlockSpec((k1, tn), lambda i, j: (0, j)),
            pl.BlockSpec((k2, tn), lambda i, j: (0, j)),
            pl.BlockSpec((tm, tn), lambda i, j: (i, j)),
            pl.BlockSpec((None, None, None, 1, tn), lambda i, j: (layer, m_gate, grp(i), 0, j)),
        ],
        out_specs=pl.BlockSpec((tm, tn), lambda i, j: (i, j)),
        out_shape=jax.ShapeDtypeStruct((t, d), F32),
        compiler_params=_params(("parallel", "arbitrary"), vmem),
        name=name,
    )(a1, a2, w1, w2, x, mod)


def _qk_post_kernel(q_ref, k_ref, v_ref, qw_ref, kw_ref, cos_ref, sin_ref,
                    qo_ref, ko_ref, vo_ref, ks_ref):
    cos, sin = cos_ref[...], sin_ref[...]
    qw, kw = qw_ref[...], kw_ref[...]
    for h in range(q_ref.shape[1] // HEAD_DIM):
        sl = slice(h * HEAD_DIM, (h + 1) * HEAD_DIM)
        y = _rope_lanes(_rmsnorm(q_ref[:, sl], qw), cos, sin, HEAD_DIM // 4)
        qo_ref[:, sl] = y.astype(BF16)
    for h in range(k_ref.shape[1] // HEAD_DIM):
        sl = slice(h * HEAD_DIM, (h + 1) * HEAD_DIM)
        y = _rope_lanes(_rmsnorm(k_ref[:, sl], kw), cos, sin, HEAD_DIM // 4)
        ks_ref[:, sl] = y
        ko_ref[:, sl] = y.astype(BF16)
    vo_ref[...] = v_ref[...].astype(BF16)


def _qk_post(proj, q_norm, k_norm, cos, sin, rows_p, seq_s):
    t = proj.shape[0]
    qw_, kvw = A_HEADS * HEAD_DIM, A_KV_HEADS * HEAD_DIM
    tm = _tile(math.gcd(rows_p, seq_s), POST_ROW_TILE)
    p_tiles, s_tiles = rows_p // tm, seq_s // tm
    kblk = qw_ // kvw

    def pos_tile(i):
        return jnp.where(i >= p_tiles, jnp.maximum(i - p_tiles, 0) % s_tiles, s_tiles)

    vmem = 2 * (tm * (qw_ + 2 * kvw) * 4 + tm * (qw_ + 2 * kvw) * 2 + tm * kvw * 4) + 4 * tm * qw_ * 4 + 4 * MIB
    return pl.pallas_call(
        _qk_post_kernel,
        grid=(t // tm,),
        in_specs=[
            pl.BlockSpec((tm, qw_), lambda i: (i, 0)),
            pl.BlockSpec((tm, kvw), lambda i: (i, kblk)),
            pl.BlockSpec((tm, kvw), lambda i: (i, kblk + 1)),
            pl.BlockSpec((1, HEAD_DIM), lambda i: (0, 0)),
            pl.BlockSpec((1, HEAD_DIM), lambda i: (0, 0)),
            pl.BlockSpec((tm, LANE), lambda i: (pos_tile(i), 0)),
            pl.BlockSpec((tm, LANE), lambda i: (pos_tile(i), 0)),
        ],
        out_specs=[
            pl.BlockSpec((tm, qw_), lambda i: (i, 0)),
            pl.BlockSpec((tm, kvw), lambda i: (i, 0)),
            pl.BlockSpec((tm, kvw), lambda i: (i, 0)),
            pl.BlockSpec((tm, kvw), lambda i: (i, 0)),
        ],
        out_shape=[
            jax.ShapeDtypeStruct((t, qw_), BF16),
            jax.ShapeDtypeStruct((t, kvw), BF16),
            jax.ShapeDtypeStruct((t, kvw), BF16),
            jax.ShapeDtypeStruct((t, kvw), F32),
        ],
        compiler_params=_params(("parallel",), vmem),
        name="even_qk_norm_rope",
    )(proj, proj, proj, q_norm.reshape(1, HEAD_DIM), k_norm.reshape(1, HEAD_DIM), cos, sin)


def _attention_kernel(*refs, group, scale, extra, aliased):
    it = iter(refs)
    q_ref, k_ref, v_ref = next(it), next(it), next(it)
    q2_ref = k2_ref = None
    if extra:
        q2_ref, k2_ref = next(it), next(it)
    if aliased:
        next(it)
    o_ref = next(it)
    k2 = k2_ref[...] if extra else None
    for j in range(k_ref.shape[1] // LANE):
        kv_sl = slice(j * LANE, (j + 1) * LANE)
        k = k_ref[:, kv_sl]
        v = v_ref[:, kv_sl]
        if extra:
            k = jnp.concatenate([k, k2], axis=-1)
        for h in range(group):
            sl = slice((j * group + h) * LANE, (j * group + h + 1) * LANE)
            q = q_ref[:, sl]
            if extra:
                q = jnp.concatenate([q, q2_ref[:, sl]], axis=-1)
            s = lax.dot_general(q, k, (((1,), (1,)), ((), ())), preferred_element_type=F32) * (scale * LOG2_E)
            m = jnp.max(s, axis=-1, keepdims=True)
            p = jnp.exp2(s - m)
            den = jnp.sum(p, axis=-1, keepdims=True)
            o = jnp.dot(p.astype(BF16), v, preferred_element_type=F32) / den
            o_ref[:, sl] = o.astype(o_ref.dtype)


def _attention(q, k, v, *, n_batch, sq, nk, q_row0, k_blk0, k_col0, v_col0, n_kv, group, scale,
               out_cols, kv_per_step=1, q2=None, q2_col0=0, k2=None, k2_blk0=0, prev=None, name):
    t = q.shape[0]
    tq = _tile(math.gcd(sq, q_row0), ATTN_Q_TILE)
    qb0 = q_row0 // tq
    qpb = sq // tq
    kps = kv_per_step
    assert n_kv % kps == 0 and k_col0 % kps == 0 and v_col0 % kps == 0 and q2_col0 % (kps * group) == 0
    kw = kps * LANE
    gw = kps * group * LANE
    kc0, vc0, q2c0 = k_col0 // kps, v_col0 // kps, q2_col0 // (kps * group)
    extra = q2 is not None
    in_specs = [
        pl.BlockSpec((tq, gw), lambda b, g, i: (qb0 + b * qpb + i, g)),
        pl.BlockSpec((nk, kw), lambda b, g, i: (k_blk0 + b, kc0 + g)),
        pl.BlockSpec((nk, kw), lambda b, g, i: (k_blk0 + b, vc0 + g)),
    ]
    args = [q, k, v]
    if extra:
        in_specs += [pl.BlockSpec((tq, gw), lambda b, g, i: (qb0 + b * qpb + i, q2c0 + g)),
                     pl.BlockSpec((nk, LANE), lambda b, g, i: (k2_blk0 + b, 0))]
        args += [q, k2]
    aliases = {}
    if prev is not None:
        in_specs.append(pl.BlockSpec(memory_space=pl.ANY))
        aliases = {len(args): 0}
        args.append(prev)
    kd = 2 * LANE if extra else LANE
    vmem = (2 * (tq * gw * 2 * (2 if extra else 1) + nk * kw * 2 * 2 + nk * LANE * 2 + tq * gw * 2)
            + nk * kd * 2 + 3 * kps * tq * nk * 4 + 4 * MIB)
    return pl.pallas_call(
        functools.partial(_attention_kernel, group=group, scale=scale, extra=extra, aliased=prev is not None),
        grid=(n_batch, n_kv // kps, qpb),
        in_specs=in_specs,
        out_specs=pl.BlockSpec((tq, gw), lambda b, g, i: (qb0 + b * qpb + i, g)),
        out_shape=jax.ShapeDtypeStruct((t, out_cols), BF16),
        input_output_aliases=aliases,
        compiler_params=_params(("parallel", "parallel", "parallel"), vmem),
        name=name,
    )(*args)


def _dft_width_kernel(f_ref, cs_ref, ab_ref):
    r = jnp.dot(f_ref[...].astype(BF16), cs_ref[...], preferred_element_type=F32)
    ab_ref[0] = r[:, :B_GROUP_W].astype(BF16)
    ab_ref[1] = r[:, B_GROUP_W:].astype(BF16)


def _dft_matrix(n):
    base = _tile(n, max(SUBLANE, int(math.sqrt(n))))
    k = jnp.arange(n, dtype=I32)

    def tables(rows):
        ang = ((rows[:, None] * k[None, :]) % n).astype(F32) * (2.0 * math.pi / n)
        return jnp.cos(ang), jnp.sin(ang)

    c1, s1 = tables(jnp.arange(n // base, dtype=I32) * base)
    c2, s2 = tables(jnp.arange(base, dtype=I32))
    q1 = jnp.concatenate([c1, s1], axis=1)[:, None, :]
    q2 = jnp.concatenate([-s1, c1], axis=1)[:, None, :]
    r1 = jnp.concatenate([c2, c2], axis=1)[None, :, :]
    r2 = jnp.concatenate([s2, s2], axis=1)[None, :, :]
    return (q1 * r1 + q2 * r2).reshape(n, 2 * n)


def _dft_tables(n):
    j = jnp.arange(n, dtype=I32)
    jk = (j[:, None] * j[None, :]) % n
    ang = jk.astype(F32) * (2.0 * math.pi / n)
    return jnp.cos(ang), jnp.sin(ang)


def _dft_width(proj, col0_blk, rows_p, seq_s):
    t = proj.shape[0]
    bw = B_GROUPS * B_GROUP_W
    cw, sw = _dft_tables(B_GROUP_W)
    cs = jnp.concatenate([cw, -sw], axis=1).astype(BF16)
    tm = _tile(math.gcd(rows_p, seq_s), 2 * ROW_TILE)
    vmem = 2 * (tm * LANE * 4 + 2 * tm * LANE * 2) + 4 * tm * LANE * 4 + 4 * MIB
    return pl.pallas_call(
        _dft_width_kernel,
        grid=(t // tm, B_GROUPS),
        in_specs=[pl.BlockSpec((tm, B_GROUP_W), lambda i, g: (i, col0_blk + g)),
                  pl.BlockSpec((B_GROUP_W, 2 * B_GROUP_W), lambda i, g: (0, 0))],
        out_specs=pl.BlockSpec((2, tm, B_GROUP_W), lambda i, g: (0, i, g)),
        out_shape=jax.ShapeDtypeStruct((2, t, bw), BF16),
        compiler_params=_params(("parallel", "parallel"), vmem),
        name="fourier_width_dft",
    )(proj, cs)


def _dft_seq(ab, n_batch, seq, row0, prev, name):
    _, t, bw = ab.shape
    f = _dft_matrix(seq).astype(BF16)
    tm = _tile(math.gcd(seq, row0), 2 * ROW_TILE)
    tk = tm
    tn = _tile(bw, 2 * COL_TILE, LANE)
    kt = seq // tk
    rb0 = row0 // tk
    ob0 = row0 // tm
    scale = 1.0 / math.sqrt(seq * B_GROUP_W)
    return _matmul(
        f, ab,
        grid=(n_batch, seq // tm, bw // tn, 2 * kt),
        a_spec=pl.BlockSpec((tm, tk), lambda b, i, j, kk: (i, kk)),
        b_spec=pl.BlockSpec((None, tk, tn), lambda b, i, j, kk: (kk // kt, rb0 + b * kt + kk % kt, j)),
        o_spec=pl.BlockSpec((tm, tn), lambda b, i, j, kk: (ob0 + b * (seq // tm) + i, j)),
        out_shape=jax.ShapeDtypeStruct((t, bw), BF16),
        tile=(tm, tn, tk), scale=scale, prev=prev, name=name)


def _kv_post_kernel(ckv_ref, kr_ref, w_ref, cos_ref, sin_ref, cn_ref, cb_ref, kr_out_ref):
    y = _rmsnorm(ckv_ref[...], w_ref[...])
    cn_ref[...] = y
    cb_ref[...] = y.astype(BF16)
    kr_out_ref[...] = _rope_lanes(kr_ref[...], cos_ref[...], sin_ref[...], QK_ROPE // 4).astype(BF16)


def _kv_post(proj, ckv_blk, kr_blk, kv_norm, cos, sin, rows_p, seq_s):
    t = proj.shape[0]
    tm = _tile(math.gcd(rows_p, seq_s), POST_ROW_TILE)
    p_tiles, s_tiles = rows_p // tm, seq_s // tm

    def pos_tile(i):
        return jnp.where(i >= p_tiles, jnp.maximum(i - p_tiles, 0) % s_tiles, s_tiles)

    vmem = 2 * (tm * KV_LORA * 10 + tm * LANE * 14) + 4 * tm * KV_LORA * 4 + 4 * MIB
    return pl.pallas_call(
        _kv_post_kernel,
        grid=(t // tm,),
        in_specs=[
            pl.BlockSpec((tm, KV_LORA), lambda i: (i, ckv_blk)),
            pl.BlockSpec((tm, LANE), lambda i: (i, kr_blk)),
            pl.BlockSpec((1, KV_LORA), lambda i: (0, 0)),
            pl.BlockSpec((tm, LANE), lambda i: (pos_tile(i), 0)),
            pl.BlockSpec((tm, LANE), lambda i: (pos_tile(i), 0)),
        ],
        out_specs=[
            pl.BlockSpec((tm, KV_LORA), lambda i: (i, 0)),
            pl.BlockSpec((tm, KV_LORA), lambda i: (i, 0)),
            pl.BlockSpec((tm, LANE), lambda i: (i, 0)),
        ],
        out_shape=[
            jax.ShapeDtypeStruct((t, KV_LORA), F32),
            jax.ShapeDtypeStruct((t, KV_LORA), BF16),
            jax.ShapeDtypeStruct((t, LANE), BF16),
        ],
        compiler_params=_params(("parallel",), vmem),
        name="odd_kv_norm_rope",
    )(proj, proj, kv_norm.reshape(1, KV_LORA), cos, sin)


def _conv_kernel(u_ref, gb_ref, gc_ref, up_ref, gcp_ref, un_ref, gcn_ref, w_ref, o_ref, *,
                 tr, rows_p, seq_p, seq_s):
    i = pl.program_id(0)
    start = i * tr
    local = jnp.where(start < rows_p, start % seq_p, jnp.maximum(start - rows_p, 0) % seq_s)
    seq = jnp.where(start < rows_p, seq_p, seq_s)
    has_prev = local > 0
    has_next = local + tr < seq
    z = gc_ref[...] * u_ref[...]
    zp = (gcp_ref[...] * up_ref[...])[SUBLANE - 1:SUBLANE, :]
    zn = (gcn_ref[...] * un_ref[...])[0:1, :]
    zp = jnp.where(has_prev, zp, 0.0)
    zn = jnp.where(has_next, zn, 0.0)
    row = lax.broadcasted_iota(I32, z.shape, 0)
    z_prev = jnp.where(row == 0, zp, pltpu.roll(z, 1, 0))
    z_next = jnp.where(row == tr - 1, zn, pltpu.roll(z, tr - 1, 0))
    w = w_ref[...]
    conv = z_prev * w[0:1, :] + z * w[1:2, :] + z_next * w[2:3, :]
    o_ref[...] = (gb_ref[...] * conv).astype(o_ref.dtype)


def _gated_conv(proj, u_blk, gb_blk, gc_blk, conv_w, rows_p, seq_p, seq_s):
    t = proj.shape[0]
    tr = _tile(math.gcd(seq_p, seq_s), POST_ROW_TILE)
    tc = _tile(D_CH, COL_TILE, LANE)
    halo = tr // SUBLANE
    last = t // SUBLANE - 1
    cpb = D_CH // tc

    def main(blk):
        return pl.BlockSpec((tr, tc), lambda i, j: (i, blk * cpb + j))

    def prev(blk):
        return pl.BlockSpec((SUBLANE, tc), lambda i, j: (jnp.maximum(i * halo - 1, 0), blk * cpb + j))

    def nxt(blk):
        return pl.BlockSpec((SUBLANE, tc), lambda i, j: (jnp.minimum((i + 1) * halo, last), blk * cpb + j))

    w8 = jnp.pad(conv_w, ((0, SUBLANE - CONV_W), (0, 0)))
    vmem = 2 * (3 * tr * tc * 4 + 4 * SUBLANE * tc * 4 + tr * tc * 2) + 8 * tr * tc * 4 + 4 * MIB
    return pl.pallas_call(
        functools.partial(_conv_kernel, tr=tr, rows_p=rows_p, seq_p=seq_p, seq_s=seq_s),
        grid=(t // tr, cpb),
        in_specs=[main(u_blk), main(gb_blk), main(gc_blk), prev(u_blk), prev(gc_blk), nxt(u_blk), nxt(gc_blk),
                  pl.BlockSpec((SUBLANE, tc), lambda i, j: (0, j))],
        out_specs=pl.BlockSpec((tr, tc), lambda i, j: (i, j)),
        out_shape=jax.ShapeDtypeStruct((t, D_CH), BF16),
        compiler_params=_params(("parallel", "parallel"), vmem),
        name="odd_gated_conv",
    )(proj, proj, proj, proj, proj, proj, proj, w8)


def _router_kernel(x_ref, nw_ref, sc_ref, sh_ref, rw_ref, rb_ref,
                   h_ref, idx_ref, wt_ref, rank_ref, cnt_ref, carry_ref):
    @pl.when(pl.program_id(0) == 0)
    def _():
        carry_ref[...] = jnp.zeros_like(carry_ref)

    h = _rmsnorm(x_ref[...], nw_ref[...]) * (1.0 + sc_ref[...]) + sh_ref[...]
    for q in range(h.shape[1] // (2 * LANE)):
        k, lanes = _pair_position(q, h_ref.shape[1])
        lo = h[:, (2 * q) * LANE:(2 * q + 1) * LANE]
        hi = h[:, (2 * q + 1) * LANE:(2 * q + 2) * LANE]
        h_ref[pl.ds(k, h.shape[0], stride=SLAB_ROWS), lanes] = _pack_pair(lo, hi)
    logits = lax.dot_general(rw_ref[...], h, (((1,), (1,)), ((), ())),
                             precision=lax.Precision.HIGHEST, preferred_element_type=F32) + rb_ref[...]
    n_e, tm = logits.shape
    eidx = lax.broadcasted_iota(I32, logits.shape, 0).astype(F32)
    work = logits
    vals, hots = [], []
    for k in range(TOP_K):
        m = jnp.max(work, axis=0, keepdims=True)
        sel = jnp.min(jnp.where(work == m, eidx, float(n_e)), axis=0, keepdims=True)
        hot = eidx == sel
        idx_ref[k:k + 1, :] = sel.astype(I32)
        vals.append(m)
        hots.append(hot)
        work = jnp.where(hot, -jnp.inf, work)
    exps = [jnp.exp(v - vals[0]) for v in vals]
    den = exps[0]
    for e in exps[1:]:
        den = den + e
    for k in range(TOP_K):
        wt_ref[k:k + 1, :] = exps[k] / den
    chosen = hots[0]
    for hot in hots[1:]:
        chosen = jnp.logical_or(chosen, hot)
    chosen = jnp.where(chosen, 1.0, 0.0)
    r = lax.broadcasted_iota(I32, (tm, tm), 0)
    c = lax.broadcasted_iota(I32, (tm, tm), 1)
    upper = jnp.where(r < c, 1.0, 0.0).astype(BF16)
    before = jnp.dot(chosen.astype(BF16), upper, preferred_element_type=F32) + carry_ref[:, 0:1]
    for k in range(TOP_K):
        rk = jnp.sum(jnp.where(hots[k], before, 0.0), axis=0, keepdims=True)
        rank_ref[k:k + 1, :] = rk.astype(I32)
    carry_ref[...] = carry_ref[...] + jnp.sum(chosen, axis=1, keepdims=True)
    cnt_ref[...] = carry_ref[...].astype(I32)


def _router(x, nw, mod, layer, router_w, router_b, rows_p, seq_s):
    t, d = x.shape
    n_e = router_w.shape[1]
    tm = _tile(math.gcd(rows_p, seq_s), ROW_TILE, LANE)
    width = _packed_width(d)

    def grp(i):
        return _group_of_row(i * tm, rows_p, seq_s)

    vmem = 2 * (tm * d * 6 + n_e * d * 4) + 4 * tm * d * 4 + tm * tm * 4 + 4 * MIB
    return pl.pallas_call(
        _router_kernel,
        grid=(t // tm,),
        in_specs=[
            pl.BlockSpec((tm, d), lambda i: (i, 0)),
            pl.BlockSpec((1, d), lambda i: (0, 0)),
            pl.BlockSpec((None, None, None, 1, d), lambda i: (layer, 4, grp(i), 0, 0)),
            pl.BlockSpec((None, None, None, 1, d), lambda i: (layer, 3, grp(i), 0, 0)),
            pl.BlockSpec((n_e, d), lambda i: (0, 0)),
            pl.BlockSpec((n_e, 1), lambda i: (0, 0)),
        ],
        out_specs=[
            pl.BlockSpec((tm * SLAB_ROWS, width), lambda i: (i, 0)),
            pl.BlockSpec((TOP_K, tm), lambda i: (0, i)),
            pl.BlockSpec((TOP_K, tm), lambda i: (0, i)),
            pl.BlockSpec((TOP_K, tm), lambda i: (0, i)),
            pl.BlockSpec((n_e, LANE), lambda i: (0, 0)),
        ],
        out_shape=[
            jax.ShapeDtypeStruct((t * SLAB_ROWS, width), jnp.uint32),
            jax.ShapeDtypeStruct((TOP_K, t), I32),
            jax.ShapeDtypeStruct((TOP_K, t), F32),
            jax.ShapeDtypeStruct((TOP_K, t), I32),
            jax.ShapeDtypeStruct((n_e, LANE), I32),
        ],
        scratch_shapes=[pltpu.VMEM((n_e, LANE), F32)],
        compiler_params=_params(("arbitrary",), vmem),
        name="moe_router",
    )(x, nw.reshape(1, d), mod, mod, router_w.T, router_b.reshape(n_e, 1))


def _slab_copy(src_hbm, row, dst, r, sem):
    return pltpu.make_async_copy(src_hbm.at[pl.ds(pl.multiple_of(row * SLAB_ROWS, SLAB_ROWS), SLAB_ROWS), :],
                                 dst.at[pl.ds(pl.multiple_of(r * SLAB_ROWS, SLAB_ROWS), SLAB_ROWS), :], sem)


def _gather_kernel(nv_ref, tok_ref, tok_next_ref, h_hbm, o_ref, buf0, buf1, sem):
    i = pl.program_id(0)
    nv = nv_ref[0]
    rows = o_ref.shape[0]
    bufs = (buf0, buf1)
    width = buf0.shape[1]

    def start_tile(tok, slot):
        def body(r2, carry):
            for u in range(2):
                r = 2 * r2 + u
                _slab_copy(h_hbm, tok[0, r], bufs[slot], r, sem.at[slot]).start(priority=u)
            return carry
        lax.fori_loop(0, rows // 2, body, 0, unroll=4)

    @pl.when(i == 0)
    def _():
        start_tile(tok_ref, 0)

    for par in (0, 1):
        @pl.when(jnp.logical_and(i % 2 == par, i + 1 < nv))
        def _():
            start_tile(tok_next_ref, 1 - par)

        @pl.when(jnp.logical_and(i % 2 == par, i < nv))
        def _():
            def body(r, carry):
                _slab_copy(h_hbm, 0, bufs[par], r, sem.at[par]).wait()
                return carry
            lax.fori_loop(0, rows, body, 0, unroll=8)
            for q in range(o_ref.shape[1] // (2 * LANE)):
                k, lanes = _pair_position(q, width)
                lo, hi = _unpack_pair(bufs[par][pl.ds(k, rows, stride=SLAB_ROWS), lanes])
                o_ref[:, (2 * q) * LANE:(2 * q + 1) * LANE] = lo.astype(o_ref.dtype)
                o_ref[:, (2 * q + 1) * LANE:(2 * q + 2) * LANE] = hi.astype(o_ref.dtype)

    @pl.when(i >= nv)
    def _():
        o_ref[...] = jnp.zeros_like(o_ref)


def _gather_rows(h_packed, d, token_of, n_valid, tg):
    nt = token_of.shape[0]
    width = _packed_width(d)
    assert tg % 2 == 0
    vmem = 2 * tg * d * 2 + 2 * tg * d * 2 + 2 * tg * d * 4 + 4 * MIB
    grid_spec = pltpu.PrefetchScalarGridSpec(
        num_scalar_prefetch=1,
        grid=(nt,),
        in_specs=[
            pl.BlockSpec((None, 1, tg), lambda i, nv: (jnp.minimum(i, nv[0] - 1), 0, 0), memory_space=pltpu.SMEM),
            pl.BlockSpec((None, 1, tg), lambda i, nv: (jnp.minimum(i + 1, nv[0] - 1), 0, 0), memory_space=pltpu.SMEM),
            pl.BlockSpec(memory_space=pl.ANY),
        ],
        out_specs=pl.BlockSpec((tg, d), lambda i, nv: (i, 0)),
        scratch_shapes=[pltpu.VMEM((tg * SLAB_ROWS, width), jnp.uint32),
                        pltpu.VMEM((tg * SLAB_ROWS, width), jnp.uint32), pltpu.SemaphoreType.DMA((2,))],
    )
    return pl.pallas_call(
        _gather_kernel,
        grid_spec=grid_spec,
        out_shape=jax.ShapeDtypeStruct((nt * tg, d), BF16),
        compiler_params=_params(("arbitrary",), vmem),
        name="moe_dispatch_gather",
    )(n_valid, token_of, token_of, h_packed)


def _gate_up_kernel(te_ref, nv_ref, x_ref, wg_ref, wu_ref, bg_ref, bu_ref, o_ref, wg_bf, wu_bf):
    r = pl.program_id(1)
    e = te_ref[r]
    e_prev = te_ref[jnp.maximum(r - 1, 0)]

    @pl.when(jnp.logical_or(r == 0, e != e_prev))
    def _():
        wg_bf[...] = wg_ref[...].astype(BF16)
        wu_bf[...] = wu_ref[...].astype(BF16)

    @pl.when(r < nv_ref[0])
    def _():
        x = x_ref[...]
        g = jnp.dot(x, wg_bf[...], preferred_element_type=F32) + bg_ref[...]
        u = jnp.dot(x, wu_bf[...], preferred_element_type=F32) + bu_ref[...]
        g = jnp.minimum(g, SWIGLU_LIMIT)
        u = jnp.clip(u, -SWIGLU_LIMIT, SWIGLU_LIMIT)
        o_ref[...] = ((u + 1.0) * g * jax.nn.sigmoid(SWIGLU_ALPHA * g)).astype(o_ref.dtype)

    @pl.when(r >= nv_ref[0])
    def _():
        o_ref[...] = jnp.zeros_like(o_ref)


def _gate_up(xs, tile_expert, n_valid, layer, w_gate, w_up, b_gate, b_up, tm):
    p, d = xs.shape
    depth, n_e, _, ff = w_gate.shape
    nt = p // tm
    nc = _tile(ff, GATE_UP_COL_TILE, LANE)

    def row(c, r, te, nv):
        return jnp.minimum(r, nv[0] - 1)

    vmem = 2 * (tm * d * 2 + 2 * d * nc * 4 + tm * nc * 2) + 2 * d * nc * 2 + 6 * tm * nc * 4 + 4 * MIB
    grid_spec = pltpu.PrefetchScalarGridSpec(
        num_scalar_prefetch=2,
        grid=(ff // nc, nt),
        in_specs=[
            pl.BlockSpec((tm, d), lambda c, r, te, nv: (row(c, r, te, nv), 0)),
            pl.BlockSpec((None, None, d, nc), lambda c, r, te, nv: (layer, te[r], 0, c)),
            pl.BlockSpec((None, None, d, nc), lambda c, r, te, nv: (layer, te[r], 0, c)),
            pl.BlockSpec((None, None, 1, nc), lambda c, r, te, nv: (layer, te[r], 0, c)),
            pl.BlockSpec((None, None, 1, nc), lambda c, r, te, nv: (layer, te[r], 0, c)),
        ],
        out_specs=pl.BlockSpec((tm, nc), lambda c, r, te, nv: (r, c)),
        scratch_shapes=[pltpu.VMEM((d, nc), BF16), pltpu.VMEM((d, nc), BF16)],
    )
    return pl.pallas_call(
        _gate_up_kernel,
        grid_spec=grid_spec,
        out_shape=jax.ShapeDtypeStruct((p, ff), BF16),
        compiler_params=_params(("arbitrary", "arbitrary"), vmem),
        name="moe_gate_up",
    )(tile_expert, n_valid, xs, w_gate, w_up, b_gate.reshape(depth, n_e, 1, ff), b_up.reshape(depth, n_e, 1, ff))


def _down_kernel(te_ref, nv_ref, a_ref, wd_ref, bd_ref, o_ref, wd_bf):
    r = pl.program_id(0)
    e = te_ref[r]
    e_prev = te_ref[jnp.maximum(r - 1, 0)]
    rows = a_ref.shape[0]

    @pl.when(jnp.logical_or(r == 0, e != e_prev))
    def _():
        wd_bf[...] = wd_ref[...].astype(BF16)

    @pl.when(r < nv_ref[0])
    def _():
        a = a_ref[...]
        for q in range(wd_bf.shape[1] // (2 * LANE)):
            sl = slice(2 * q * LANE, (2 * q + 2) * LANE)
            y = jnp.dot(a, wd_bf[:, sl], preferred_element_type=F32) + bd_ref[:, sl]
            k, lanes = _pair_position(q, o_ref.shape[1])
            o_ref[pl.ds(k, rows, stride=SLAB_ROWS), lanes] = _pack_pair(y[:, :LANE], y[:, LANE:])

    @pl.when(r >= nv_ref[0])
    def _():
        o_ref[...] = jnp.zeros_like(o_ref)


def _down(act, tile_expert, n_valid, layer, w_down, b_down, tm):
    p, ff = act.shape
    depth, n_e, _, d = w_down.shape
    nt = p // tm
    width = _packed_width(d)
    vmem = 2 * (tm * ff * 2 + tm * d * 2 + ff * d * 4) + ff * d * 2 + 4 * tm * 2 * LANE * 4 + 4 * MIB
    grid_spec = pltpu.PrefetchScalarGridSpec(
        num_scalar_prefetch=2,
        grid=(nt,),
        in_specs=[
            pl.BlockSpec((tm, ff), lambda r, te, nv: (jnp.minimum(r, nv[0] - 1), 0)),
            pl.BlockSpec((None, None, ff, d), lambda r, te, nv: (layer, te[r], 0, 0)),
            pl.BlockSpec((None, None, 1, d), lambda r, te, nv: (layer, te[r], 0, 0)),
        ],
        out_specs=pl.BlockSpec((tm * SLAB_ROWS, width), lambda r, te, nv: (r, 0)),
        scratch_shapes=[pltpu.VMEM((ff, d), BF16)],
    )
    return pl.pallas_call(
        _down_kernel,
        grid_spec=grid_spec,
        out_shape=jax.ShapeDtypeStruct((p * SLAB_ROWS, width), jnp.uint32),
        compiler_params=_params(("arbitrary",), vmem),
        name="moe_down",
    )(tile_expert, n_valid, act, w_down, b_down.reshape(depth, n_e, 1, d))


def _combine_kernel(pos_ref, pos_next_ref, x_ref, wt_ref, g_ref, fw_ref, y_hbm, *rest, p_tiles):
    if p_tiles is None:
        o_ref, buf0, buf1, sem = rest
    else:
        oc_ref, ol_ref, buf0, buf1, sem, o_ref = rest
    i = pl.program_id(0)
    n = pl.num_programs(0)
    rows = x_ref.shape[0]
    bufs = (buf0, buf1)
    width = buf0.shape[1]

    def start_tile(pos, slot):
        def body(r, carry):
            for k in range(TOP_K):
                _slab_copy(y_hbm, pos[k, r], bufs[slot], k * rows + r, sem.at[slot]).start(priority=k % 2)
            return carry
        lax.fori_loop(0, rows, body, 0, unroll=4)

    @pl.when(i == 0)
    def _():
        start_tile(pos_ref, 0)

    for par in (0, 1):
        @pl.when(jnp.logical_and(i % 2 == par, i + 1 < n))
        def _():
            start_tile(pos_next_ref, 1 - par)

        @pl.when(i % 2 == par)
        def _():
            def wait_body(r, carry):
                for k in range(TOP_K):
                    _slab_copy(y_hbm, 0, bufs[par], k * rows + r, sem.at[par]).wait()
                return carry
            lax.fori_loop(0, rows, wait_body, 0, unroll=4)
            wt = wt_ref[...]
            wb = [jnp.broadcast_to(wt[:, k:k + 1], (rows, LANE)) for k in range(TOP_K)]
            for c in range(x_ref.shape[1] // (2 * LANE)):
                row, lanes = _pair_position(c, width)
                acc_lo = acc_hi = None
                for k in range(TOP_K):
                    lo, hi = _unpack_pair(
                        bufs[par][pl.ds(k * rows * SLAB_ROWS + row, rows, stride=SLAB_ROWS), lanes])
                    acc_lo = wb[k] * lo if acc_lo is None else acc_lo + wb[k] * lo
                    acc_hi = wb[k] * hi if acc_hi is None else acc_hi + wb[k] * hi
                s_lo = slice((2 * c) * LANE, (2 * c + 1) * LANE)
                s_hi = slice((2 * c + 1) * LANE, (2 * c + 2) * LANE)
                o_ref[:, s_lo] = x_ref[:, s_lo] + g_ref[:, s_lo] * acc_lo
                o_ref[:, s_hi] = x_ref[:, s_hi] + g_ref[:, s_hi] * acc_hi

    if p_tiles is not None:
        @pl.when(i < p_tiles)
        def _():
            oc_ref[...] = _rmsnorm(o_ref[...], fw_ref[...])

        @pl.when(i >= p_tiles)
        def _():
            ol_ref[...] = _rmsnorm(o_ref[...], fw_ref[...])


def _combine(x, y_pairs, pos, wts, mod, layer, final_w, rows_p, seq_s, *, final_norm):
    t, d = x.shape
    tc = _tile(math.gcd(rows_p, seq_s), COMBINE_ROW_TILE)
    nt = t // tc
    pos3 = pos.reshape(TOP_K, nt, tc).transpose(1, 0, 2)

    def grp(i):
        return _group_of_row(i * tc, rows_p, seq_s)

    width = _packed_width(d)
    vmem = 2 * TOP_K * tc * d * 2 + 2 * (3 * tc * d * 4 + tc * LANE * 4) + 4 * tc * d * 4 + 4 * MIB
    scratch = [pltpu.VMEM((TOP_K * tc * SLAB_ROWS, width), jnp.uint32),
               pltpu.VMEM((TOP_K * tc * SLAB_ROWS, width), jnp.uint32), pltpu.SemaphoreType.DMA((2,))]
    if final_norm:
        p_tiles = rows_p // tc
        out_specs = [pl.BlockSpec((tc, d), lambda i: (jnp.minimum(i, p_tiles - 1), 0)),
                     pl.BlockSpec((tc, d), lambda i: (jnp.maximum(i - p_tiles, 0), 0))]
        out_shape = [jax.ShapeDtypeStruct((rows_p, d), F32), jax.ShapeDtypeStruct((t - rows_p, d), F32)]
        scratch.append(pltpu.VMEM((tc, d), F32))
    else:
        p_tiles = None
        out_specs = pl.BlockSpec((tc, d), lambda i: (i, 0))
        out_shape = jax.ShapeDtypeStruct((t, d), F32)
    return pl.pallas_call(
        functools.partial(_combine_kernel, p_tiles=p_tiles),
        grid=(nt,),
        in_specs=[
            pl.BlockSpec((None, TOP_K, tc), lambda i: (i, 0, 0), memory_space=pltpu.SMEM),
            pl.BlockSpec((None, TOP_K, tc), lambda i: (jnp.minimum(i + 1, nt - 1), 0, 0), memory_space=pltpu.SMEM),
            pl.BlockSpec((tc, d), lambda i: (i, 0)),
            pl.BlockSpec((tc, TOP_K), lambda i: (i, 0)),
            pl.BlockSpec((None, None, None, 1, d), lambda i: (layer, 5, grp(i), 0, 0)),
            pl.BlockSpec((1, d), lambda i: (0, 0)),
            pl.BlockSpec(memory_space=pl.ANY),
        ],
        out_specs=out_specs,
        out_shape=out_shape,
        scratch_shapes=scratch,
        compiler_params=_params(("arbitrary",), vmem),
        name="moe_combine",
    )(pos3, pos3, x, wts.T, mod, final_w.reshape(1, d), y_pairs)


def _moe_layer(x, nw, mod, layer, router_w, router_b, w_gate, b_gate, w_up, b_up, w_down, b_down,
               final_w, rows_p, seq_s, *, final_norm):
    t, d = x.shape
    n_e = router_w.shape[1]
    tm = EXPERT_ROW_TILE
    tg = _tile(tm, GATHER_ROW_TILE)
    h, idx, wts, rank, cnt = _router(x, nw, mod, layer, router_w, router_b, rows_p, seq_s)
    counts = cnt[:, 0]
    tiles_e = (counts + tm - 1) // tm
    e_ids = jnp.arange(n_e, dtype=I32)
    tile_end = jnp.sum(jnp.where(e_ids[None, :] <= e_ids[:, None], tiles_e[None, :], 0), axis=1)
    offs = (tile_end - tiles_e) * tm
    pos = jnp.sum(jnp.where(idx[:, :, None] == e_ids[None, None, :], offs[None, None, :], 0), axis=-1) + rank
    n_tiles = (t * TOP_K + tm - 1) // tm + n_e
    p_rows = n_tiles * tm
    n_valid_tiles = tile_end[-1]
    tok = jnp.broadcast_to(jnp.arange(t, dtype=I32)[None, :], (TOP_K, t))
    token_of = jnp.zeros((p_rows,), I32).at[pos.reshape(-1)].set(tok.reshape(-1))
    tile_ids = jnp.minimum(jnp.arange(n_tiles, dtype=I32), n_valid_tiles - 1)
    tile_expert = jnp.sum((tile_end[None, :] <= tile_ids[:, None]).astype(I32), axis=1)
    nv_m = n_valid_tiles.reshape(1).astype(I32)
    nv_g = (n_valid_tiles * (tm // tg)).reshape(1).astype(I32)
    xs = _gather_rows(h, d, token_of.reshape(p_rows // tg, 1, tg), nv_g, tg)
    act = _gate_up(xs, tile_expert, nv_m, layer, w_gate, w_up, b_gate, b_up, tm)
    y_pairs = _down(act, tile_expert, nv_m, layer, w_down, b_down, tm)
    return _combine(x, y_pairs, pos, wts, mod, layer, final_w, rows_p, seq_s, final_norm=final_norm)


def _even_mixer(x, mod, layer, nw, w_in, q_norm, k_norm, w_out, cache_k, cache_v, dims):
    rows_p, seq_p, n_bp, seq_s, n_bs = dims
    t = x.shape[0]
    qw_, kvw = A_HEADS * HEAD_DIM, A_KV_HEADS * HEAD_DIM
    past = cache_k.shape[1]
    nk_s = past + seq_s
    proj = _norm_matmul(x, 0, nw, w_in.astype(BF16), rows_p, seq_s, mod=mod, layer=layer,
                        m_shift=0, m_scale=1, name="even_in_proj")
    tm_post = _tile(math.gcd(rows_p, seq_s), POST_ROW_TILE)
    cos, sin = _rope_tables(seq_s, HEAD_DIM, tm_post)
    qh, kh, vh, kst = _qk_post(proj, q_norm, k_norm, cos, sin, rows_p, seq_s)
    scale = HEAD_DIM ** -0.5
    group = A_HEADS // A_KV_HEADS
    attn = _attention(qh, kh, vh, n_batch=n_bp, sq=seq_p, nk=seq_p, q_row0=0, k_blk0=0, k_col0=0, v_col0=0,
                      n_kv=A_KV_HEADS, group=group, scale=scale, out_cols=qw_, prev=jnp.zeros((t, qw_), BF16),
                      kv_per_step=_tile(A_KV_HEADS, CTX_KV_HEADS_PER_STEP, 1), name="even_attention_ctx")
    k_lat = kh[rows_p:].reshape(n_bs, seq_s, kvw)
    v_lat = vh[rows_p:].reshape(n_bs, seq_s, kvw)
    k_all = jnp.concatenate([cache_k.reshape(n_bs, past, kvw).astype(BF16), k_lat], axis=1).reshape(n_bs * nk_s, kvw)
    v_all = jnp.concatenate([cache_v.reshape(n_bs, past, kvw).astype(BF16), v_lat], axis=1).reshape(n_bs * nk_s, kvw)
    attn = _attention(qh, k_all, v_all, n_batch=n_bs, sq=seq_s, nk=nk_s, q_row0=rows_p, k_blk0=0, k_col0=0,
                      v_col0=0, n_kv=A_KV_HEADS, group=group, scale=scale, out_cols=qw_, prev=attn,
                      name="even_attention_latent")
    ab = _dft_width(proj, (qw_ + 2 * kvw) // B_GROUP_W, rows_p, seq_s)
    four = _dft_seq(ab, n_bp, seq_p, 0, jnp.zeros((t, B_GROUPS * B_GROUP_W), BF16), "fourier_seq_dft_ctx")
    four = _dft_seq(ab, n_bs, seq_s, rows_p, four, "fourier_seq_dft_latent")
    w_out_bf = w_out.astype(BF16)
    x_new = _out_proj(attn, four, w_out_bf[:qw_], w_out_bf[qw_:], x, mod, layer, 2, rows_p, seq_s,
                      name="even_out_proj")
    k_state = kst[:rows_p].reshape(n_bp, seq_p, A_KV_HEADS, HEAD_DIM)
    v_state = proj[:rows_p, qw_ + kvw:qw_ + 2 * kvw].reshape(n_bp, seq_p, A_KV_HEADS, HEAD_DIM)
    return x_new, k_state, v_state


def _odd_mixer(x, mod, layer, nw, w_in, q_a_norm, kv_a_norm, w_uq, w_ukv, conv_w, w_out,
               cache_ckv, cache_kr, dims):
    rows_p, seq_p, n_bp, seq_s, n_bs = dims
    d = x.shape[1]
    past = cache_ckv.shape[1]
    nk_s = past + seq_s
    o1, o2, o3 = Q_LORA, Q_LORA + KV_LORA, Q_LORA + KV_LORA + QK_ROPE
    o4, o5 = o3 + D_CH, o3 + 2 * D_CH
    pad_w = LANE - QK_ROPE
    w_in_r = jnp.concatenate([w_in[:, o3:o4], w_in[:, o4:o5], w_in[:, o5:], w_in[:, o1:o2], w_in[:, :o1],
                              w_in[:, o2:o3], jnp.zeros((d, pad_w), w_in.dtype)], axis=1).astype(BF16)
    c_ckv = 3 * D_CH
    c_cq = c_ckv + KV_LORA
    c_kr = c_cq + Q_LORA
    assert c_ckv % KV_LORA == 0 and c_cq % Q_LORA == 0 and c_kr % LANE == 0
    proj = _norm_matmul(x, 0, nw, w_in_r, rows_p, seq_s, mod=mod, layer=layer, m_shift=0, m_scale=1,
                        name="odd_in_proj")
    hq = QK_NOPE + QK_ROPE
    wq = w_uq.reshape(Q_LORA, C_HEADS, hq)
    wq_rope = jnp.pad(wq[:, :, QK_NOPE:], ((0, 0), (0, 0), (0, pad_w)))
    wq_r = jnp.concatenate([wq[:, :, :QK_NOPE].reshape(Q_LORA, C_HEADS * QK_NOPE),
                            wq_rope.reshape(Q_LORA, C_HEADS * LANE)], axis=1).astype(BF16)
    tm = _tile(math.gcd(rows_p, seq_s), ROW_TILE)
    tn = _tile(wq_r.shape[1], IN_PROJ_COL_TILE, LANE)
    assert (C_HEADS * QK_NOPE) % tn == 0
    cos_q, sin_q = _rope_tables(seq_s, QK_ROPE, tm)
    q = _norm_matmul(proj, c_cq // Q_LORA, q_a_norm, wq_r, rows_p, seq_s,
                     rope=(cos_q, sin_q, QK_ROPE // 4, (C_HEADS * QK_NOPE) // tn), out_dtype=BF16,
                     name="odd_q_up_proj")
    tm_post = _tile(math.gcd(rows_p, seq_s), POST_ROW_TILE)
    cos_k, sin_k = _rope_tables(seq_s, QK_ROPE, tm_post)
    ckvn, ckvn_bf, kr = _kv_post(proj, c_ckv // KV_LORA, c_kr // LANE, kv_a_norm, cos_k, sin_k, rows_p, seq_s)
    hkv = QK_NOPE + V_HEAD
    wkv = w_ukv.reshape(KV_LORA, C_HEADS, hkv)
    wkv_r = jnp.concatenate([wkv[:, :, :QK_NOPE].reshape(KV_LORA, C_HEADS * QK_NOPE),
                             wkv[:, :, QK_NOPE:].reshape(KV_LORA, C_HEADS * V_HEAD)], axis=1).astype(BF16)
    n_kv_cols = wkv_r.shape[1]
    ckv_all = jnp.concatenate([cache_ckv.astype(BF16), ckvn_bf[rows_p:].reshape(n_bs, seq_s, KV_LORA)],
                              axis=1).reshape(n_bs * nk_s, KV_LORA)
    kr_cache = jnp.pad(cache_kr, ((0, 0), (0, 0), (0, pad_w))).astype(BF16)
    kr_all = jnp.concatenate([kr_cache, kr[rows_p:].reshape(n_bs, seq_s, LANE)], axis=1).reshape(n_bs * nk_s, LANE)

    def kv_up(lat, name):
        m = lat.shape[0]
        tmm = _tile(m, ROW_TILE)
        tnn = _tile(n_kv_cols, 2 * COL_TILE, LANE)
        return _matmul(lat, wkv_r, grid=(1, m // tmm, n_kv_cols // tnn, 1),
                       a_spec=pl.BlockSpec((tmm, KV_LORA), lambda b, i, j, kk: (i, 0)),
                       b_spec=pl.BlockSpec((KV_LORA, tnn), lambda b, i, j, kk: (0, j)),
                       o_spec=pl.BlockSpec((tmm, tnn), lambda b, i, j, kk: (i, j)),
                       out_shape=jax.ShapeDtypeStruct((m, n_kv_cols), BF16),
                       tile=(tmm, tnn, KV_LORA), name=name)

    kv_p = kv_up(ckvn_bf[:rows_p], "odd_kv_up_proj_ctx")
    kv_s = kv_up(ckv_all, "odd_kv_up_proj_latent")
    scale = hq ** -0.5
    out_cols = C_HEADS * V_HEAD
    attn = _attention(q, kv_p, kv_p, n_batch=n_bp, sq=seq_p, nk=seq_p, q_row0=0, k_blk0=0, k_col0=0,
                      v_col0=C_HEADS, n_kv=C_HEADS, group=1, scale=scale, out_cols=out_cols,
                      q2=q, q2_col0=C_HEADS, k2=kr, k2_blk0=0, prev=jnp.zeros((x.shape[0], out_cols), BF16),
                      kv_per_step=_tile(C_HEADS, CTX_KV_HEADS_PER_STEP, 1), name="odd_attention_ctx")
    attn = _attention(q, kv_s, kv_s, n_batch=n_bs, sq=seq_s, nk=nk_s, q_row0=rows_p, k_blk0=0, k_col0=0,
                      v_col0=C_HEADS, n_kv=C_HEADS, group=1, scale=scale, out_cols=out_cols,
                      q2=q, q2_col0=C_HEADS, k2=kr_all, k2_blk0=0, prev=attn,
                      kv_per_step=_tile(C_HEADS, LATENT_MLA_HEADS_PER_STEP, 1), name="odd_attention_latent")
    conv = _gated_conv(proj, 0, 1, 2, conv_w, rows_p, seq_p, seq_s)
    w_out_bf = w_out.astype(BF16)
    x_new = _out_proj(attn, conv, w_out_bf[:out_cols], w_out_bf[out_cols:], x, mod, layer, 2, rows_p, seq_s,
                      name="odd_out_proj")
    ckv_state = ckvn[:rows_p].reshape(n_bp, seq_p, KV_LORA)
    kr_state = proj[:rows_p, c_kr:c_kr + QK_ROPE].reshape(n_bp, seq_p, QK_ROPE)
    return x_new, ckv_state, kr_state


def kernel(x_prompt, x_sample, cache_attn_k, cache_attn_v, cache_mla_ckv, cache_mla_krope, c, c_ctx,
           norm1_w, norm2_w, w_ada, b_ada,
           even_w_in, even_q_norm, even_k_norm, even_w_out,
           odd_w_in, odd_q_a_norm, odd_kv_a_norm, odd_w_uq, odd_w_ukv, odd_conv_w, odd_w_out,
           router_w, router_b, w_gate, b_gate, w_up, b_up, w_down, b_down, final_norm_w):
    n_bp, seq_p, d = x_prompt.shape
    n_bs, seq_s, _ = x_sample.shape
    depth = w_ada.shape[0]
    rows_p = n_bp * seq_p
    dims = (rows_p, seq_p, n_bp, seq_s, n_bs)
    n_groups = 1 + n_bs
    g8 = -(-n_groups // SUBLANE) * SUBLANE
    cond = jnp.concatenate([c_ctx[None, :], c, jnp.zeros((g8 - n_groups, d), F32)], axis=0)
    ada = _ada_modulation(cond, w_ada, b_ada)
    mod = ada[:, :n_groups].reshape(depth, n_groups, N_MOD, d).transpose(0, 2, 1, 3)[:, :, :, None, :]
    x = jnp.concatenate([x_prompt.reshape(rows_p, d), x_sample.reshape(n_bs * seq_s, d)], axis=0)
    st_k, st_v, st_ckv, st_kr = [], [], [], []
    for l in range(depth):
        j = l // 2
        if l % 2 == 0:
            x, ks, vs = _even_mixer(x, mod, l, norm1_w[l], even_w_in[j], even_q_norm[j], even_k_norm[j],
                                    even_w_out[j], cache_attn_k[:, j], cache_attn_v[:, j], dims)
            st_k.append(ks)
            st_v.append(vs)
        else:
            x, cs, rs = _odd_mixer(x, mod, l, norm1_w[l], odd_w_in[j], odd_q_a_norm[j], odd_kv_a_norm[j],
                                   odd_w_uq[j], odd_w_ukv[j], odd_conv_w[j], odd_w_out[j],
                                   cache_mla_ckv[:, j], cache_mla_krope[:, j], dims)
            st_ckv.append(cs)
            st_kr.append(rs)
        x = _moe_layer(x, norm2_w[l], mod, l, router_w[l], router_b[l], w_gate, b_gate, w_up, b_up,
                       w_down, b_down, final_norm_w, rows_p, seq_s, final_norm=(l == depth - 1))
    y_ctx, y_lat = x
    y_prompt = y_ctx.reshape(n_bp, seq_p, d)
    y_sample = y_lat.reshape(n_bs, seq_s, d)
    return (y_prompt, y_sample, jnp.stack(st_k, axis=1), jnp.stack(st_v, axis=1),
            jnp.stack(st_ckv, axis=1), jnp.stack(st_kr, axis=1))
```

```python
import functools
import math

import jax
import jax.numpy as jnp
from jax import lax
from jax.experimental import pallas as pl
from jax.experimental.pallas import tpu as pltpu

F32 = jnp.float32
BF16 = jnp.bfloat16
I32 = jnp.int32

GRID_W = 64
HEAD_DIM = 128
ROPE_THETA = 10000.0
NORM_EPS = 1e-6
A_HEADS = 24
A_KV_HEADS = 6
B_GROUPS = 8
B_GROUP_W = 128
C_HEADS = 24
Q_LORA = 896
KV_LORA = 512
QK_NOPE = 128
QK_ROPE = 64
V_HEAD = 128
D_CH = 1024
CONV_W = 3
N_EXPERTS = 32
TOP_K = 4
EXPERT_FF = 1024
SWIGLU_LIMIT = 7.0
SWIGLU_ALPHA = 1.702
N_MOD = 6
LOG2_E = math.log2(math.e)

LANE = 128
SUBLANE = 8
D_MODEL = 4096
SLAB_ROWS = D_MODEL // (2 * LANE)
V7X_VMEM_BYTES = 64 * 2**20
MIB = 2**20

ROW_TILE = 512
COL_TILE = 512
IN_PROJ_COL_TILE = 1408
NORM_ROW_CHUNK = 64
POST_ROW_TILE = 256
ATTN_Q_TILE = 256
CTX_KV_HEADS_PER_STEP = 8
LATENT_MLA_HEADS_PER_STEP = 4
LATENT_GQA_KV_HEADS_PER_STEP = 2
ADA_COL_TILE = 512
EXPERT_ROW_TILE = 512
GATE_UP_COL_TILE = 512
GATHER_ROW_TILE = 256
COMBINE_ROW_TILE = 128


def _tile(n, pref, align=SUBLANE):
    t = min(pref, n)
    t -= t % align
    while t > align and n % t:
        t -= align
    assert t >= align and n % t == 0, (n, pref, align)
    return t


def _params(sem, vmem_bytes):
    limit = int(min(max(vmem_bytes, 16 * MIB), V7X_VMEM_BYTES - 6 * MIB))
    return pltpu.CompilerParams(dimension_semantics=sem, vmem_limit_bytes=limit)


def _group_of_row(start, rows_p, seq_s):
    return jnp.where(start < rows_p, 0, 1 + jnp.maximum(start - rows_p, 0) // seq_s)


def _ada_kernel(c_ref, w_ref, b_ref, o_ref):
    c = c_ref[...]
    s = (c * jax.nn.sigmoid(c)).astype(BF16)
    o_ref[...] = jnp.dot(s, w_ref[...].astype(BF16), preferred_element_type=F32) + b_ref[...]


def _ada_modulation(cond, w_ada, b_ada):
    depth, d, n = w_ada.shape
    g8 = cond.shape[0]
    tn = _tile(n, ADA_COL_TILE, LANE)
    vmem = 2 * (d * tn * 4) + d * tn * 2 + 4 * MIB
    return pl.pallas_call(
        _ada_kernel,
        grid=(depth, n // tn),
        in_specs=[
            pl.BlockSpec((g8, d), lambda l, j: (0, 0)),
            pl.BlockSpec((None, d, tn), lambda l, j: (l, 0, j)),
            pl.BlockSpec((None, 1, tn), lambda l, j: (l, 0, j)),
        ],
        out_specs=pl.BlockSpec((None, g8, tn), lambda l, j: (l, 0, j)),
        out_shape=jax.ShapeDtypeStruct((depth, g8, n), F32),
        compiler_params=_params(("parallel", "parallel"), vmem),
        name="ada_modulation",
    )(cond, w_ada, b_ada.reshape(depth, 1, n))


def _rmsnorm(x, w):
    return x * lax.rsqrt(jnp.mean(x * x, axis=-1, keepdims=True) + NORM_EPS) * w


def _pack_pair(lo, hi):
    lo_b = lax.bitcast_convert_type(lo.astype(jnp.bfloat16).astype(F32), jnp.uint32)
    hi_b = lax.bitcast_convert_type(hi.astype(jnp.bfloat16).astype(F32), jnp.uint32)
    return hi_b | (lo_b >> 16)


def _packed_width(d):
    assert d == 2 * LANE * SLAB_ROWS, (d, SLAB_ROWS)
    return LANE


def _pair_position(q, width):
    assert width == LANE
    return q, slice(0, LANE)


def _unpack_pair(w):
    lo = lax.bitcast_convert_type(w << 16, F32)
    hi = lax.bitcast_convert_type(w & jnp.uint32(0xFFFF0000), F32)
    return lo, hi


def _rope_lanes(y, cos, sin, half):
    lane = lax.broadcasted_iota(I32, y.shape, 1)
    partner = jnp.where(lane % (2 * half) < half,
                        pltpu.roll(y, LANE - half, 1),
                        pltpu.roll(y, half, 1))
    return y * cos + partner * sin


def _rope_tables(n_tokens, rot_dim, ident_rows):
    rows = n_tokens // GRID_W
    row = jnp.repeat(jnp.arange(rows), GRID_W).astype(F32)
    col = jnp.tile(jnp.arange(GRID_W), rows).astype(F32)
    half = rot_dim // 2
    inv = ROPE_THETA ** (-jnp.arange(0, half, 2, dtype=F32) / half)
    ang_r = row[:, None] * inv[None, :]
    ang_c = col[:, None] * inv[None, :]
    ang = jnp.concatenate([ang_r, ang_r, ang_c, ang_c], axis=-1)
    sign = jnp.tile(jnp.concatenate([-jnp.ones((half // 2,), F32), jnp.ones((half // 2,), F32)]), 2)
    cos = jnp.cos(ang)
    sin = jnp.sin(ang) * sign[None, :]
    pad = LANE - rot_dim
    cos = jnp.pad(cos, ((0, ident_rows), (0, pad)), constant_values=1.0)
    sin = jnp.pad(sin, ((0, ident_rows), (0, pad)), constant_values=0.0)
    return cos, sin


def _norm_matmul_kernel(*refs, modulate, rope_half):
    it = iter(refs)
    x_ref, nw_ref = next(it), next(it)
    sc_ref = sh_ref = cos_ref = sin_ref = None
    if modulate:
        sc_ref, sh_ref = next(it), next(it)
    w_ref = next(it)
    if rope_half:
        cos_ref, sin_ref = next(it), next(it)
    o_ref, h_ref = next(it), next(it)

    @pl.when(pl.program_id(1) == 0)
    def _():
        rc = _tile(x_ref.shape[0], NORM_ROW_CHUNK, 2 * SUBLANE)

        def chunk(r, carry):
            rows = pl.ds(pl.multiple_of(r * rc, rc), rc)
            y = _rmsnorm(x_ref[rows, :], nw_ref[...])
            if modulate:
                y = y * (1.0 + sc_ref[...]) + sh_ref[...]
            h_ref[rows, :] = y.astype(BF16)
            return carry
        lax.fori_loop(0, x_ref.shape[0] // rc, chunk, 0)

    acc = jnp.dot(h_ref[...], w_ref[...], preferred_element_type=F32)
    if rope_half:
        cos, sin = cos_ref[...], sin_ref[...]
        for c in range(acc.shape[1] // LANE):
            sl = slice(c * LANE, (c + 1) * LANE)
            o_ref[:, sl] = _rope_lanes(acc[:, sl], cos, sin, rope_half).astype(o_ref.dtype)
    else:
        o_ref[...] = acc.astype(o_ref.dtype)


def _norm_matmul(x, x_col, nw, w, rows_p, seq_s, *, mod=None, layer=0, m_shift=0, m_scale=0,
                 rope=None, out_dtype=F32, name):
    t = x.shape[0]
    k, n = w.shape
    tm = _tile(math.gcd(rows_p, seq_s), ROW_TILE)
    tn = _tile(n, IN_PROJ_COL_TILE, LANE)
    p_tiles = rows_p // tm
    s_tiles = seq_s // tm

    def grp(i):
        return _group_of_row(i * tm, rows_p, seq_s)

    in_specs = [pl.BlockSpec((tm, k), lambda i, j: (i, x_col)),
                pl.BlockSpec((1, k), lambda i, j: (0, 0))]
    args = [x, nw.reshape(1, k)]
    if mod is not None:
        in_specs += [pl.BlockSpec((None, None, None, 1, k), lambda i, j: (layer, m_scale, grp(i), 0, 0)),
                     pl.BlockSpec((None, None, None, 1, k), lambda i, j: (layer, m_shift, grp(i), 0, 0))]
        args += [mod, mod]
    in_specs.append(pl.BlockSpec((k, tn), lambda i, j: (0, j)))
    args.append(w)
    rope_half = 0
    if rope is not None:
        cos, sin, rope_half, first_tile = rope

        def pos_tile(i, j):
            is_pos = jnp.logical_and(i >= p_tiles, j >= first_tile)
            return jnp.where(is_pos, jnp.maximum(i - p_tiles, 0) % s_tiles, s_tiles)

        in_specs += [pl.BlockSpec((tm, LANE), lambda i, j: (pos_tile(i, j), 0)),
                     pl.BlockSpec((tm, LANE), lambda i, j: (pos_tile(i, j), 0))]
        args += [cos, sin]
    osz = jnp.dtype(out_dtype).itemsize
    vmem = 2 * (tm * k * 4 + k * tn * 2 + tm * tn * osz) + tm * k * 2 + 2 * tm * tn * 4 + 4 * MIB
    return pl.pallas_call(
        functools.partial(_norm_matmul_kernel, modulate=mod is not None, rope_half=rope_half),
        grid=(t // tm, n // tn),
        in_specs=in_specs,
        out_specs=pl.BlockSpec((tm, tn), lambda i, j: (i, j)),
        out_shape=jax.ShapeDtypeStruct((t, n), out_dtype),
        scratch_shapes=[pltpu.VMEM((tm, k), BF16)],
        compiler_params=_params(("parallel", "arbitrary"), vmem),
        name=name,
    )(*args)


def _matmul_kernel(*refs, scale, aliased):
    a_ref, b_ref = refs[0], refs[1]
    o_ref, acc_ref = refs[-2], refs[-1]
    del aliased
    kk = pl.program_id(3)

    @pl.when(kk == 0)
    def _():
        acc_ref[...] = jnp.zeros_like(acc_ref)

    acc_ref[...] += jnp.dot(a_ref[...], b_ref[...], preferred_element_type=F32)

    @pl.when(kk == pl.num_programs(3) - 1)
    def _():
        acc = acc_ref[...]
        if scale != 1.0:
            acc = acc * scale
        o_ref[...] = acc.astype(o_ref.dtype)


def _matmul(a, b, *, grid, a_spec, b_spec, o_spec, out_shape, tile, scale=1.0, prev=None, name):
    tm, tn, tk = tile
    in_specs = [a_spec, b_spec]
    args = [a, b]
    aliases = {}
    if prev is not None:
        in_specs.append(pl.BlockSpec(memory_space=pl.ANY))
        args.append(prev)
        aliases = {2: 0}
    osz = jnp.dtype(out_shape.dtype).itemsize
    vmem = 2 * (tm * tk * 2 + tk * tn * 2 + tm * tn * osz) + 2 * tm * tn * 4 + 4 * MIB
    return pl.pallas_call(
        functools.partial(_matmul_kernel, scale=scale, aliased=prev is not None),
        grid=grid,
        in_specs=in_specs,
        out_specs=o_spec,
        out_shape=out_shape,
        scratch_shapes=[pltpu.VMEM((tm, tn), F32)],
        input_output_aliases=aliases,
        compiler_params=_params(("parallel", "parallel", "parallel", "arbitrary"), vmem),
        name=name,
    )(*args)


def _out_proj_kernel(a1_ref, a2_ref, w1_ref, w2_ref, x_ref, g_ref, o_ref):
    acc = jnp.dot(a1_ref[...], w1_ref[...], preferred_element_type=F32)
    acc += jnp.dot(a2_ref[...], w2_ref[...], preferred_element_type=F32)
    o_ref[...] = x_ref[...] + g_ref[...] * acc


def _out_proj(a1, a2, w1, w2, x, mod, layer, m_gate, rows_p, seq_s, *, name):
    t, d = x.shape
    k1, k2 = a1.shape[1], a2.shape[1]
    tm = _tile(math.gcd(rows_p, seq_s), ROW_TILE)
    tn = _tile(d, 2 * COL_TILE, LANE)

    def grp(i):
        return _group_of_row(i * tm, rows_p, seq_s)

    vmem = 2 * (tm * (k1 + k2) * 2 + (k1 + k2) * tn * 2 + 2 * tm * tn * 4) + 2 * tm * tn * 4 + 4 * MIB
    return pl.pallas_call(
        _out_proj_kernel,
        grid=(t // tm, d // tn),
        in_specs=[
            pl.BlockSpec((tm, k1), lambda i, j: (i, 0)),
            pl.BlockSpec((tm, k2), lambda i, j: (i, 0)),
            pl.BlockSpec((k1, tn), lambda i, j: (0, j)),
            pl.BlockSpec((k2, tn), lambda i, j: (0, j)),
            pl.BlockSpec((tm, tn), lambda i, j: (i, j)),
            pl.BlockSpec((None, None, None, 1, tn), lambda i, j: (layer, m_gate, grp(i), 0, j)),
        ],
        out_specs=pl.BlockSpec((tm, tn), lambda i, j: (i, j)),
        out_shape=jax.ShapeDtypeStruct((t, d), F32),
        compiler_params=_params(("parallel", "arbitrary"), vmem),
        name=name,
    )(a1, a2, w1, w2, x, mod)


def _qk_post_kernel(q_ref, k_ref, v_ref, qw_ref, kw_ref, cos_ref, sin_ref,
                    qo_ref, ko_ref, vo_ref, ks_ref):
    cos, sin = cos_ref[...], sin_ref[...]
    qw, kw = qw_ref[...], kw_ref[...]
    for h in range(q_ref.shape[1] // HEAD_DIM):
        sl = slice(h * HEAD_DIM, (h + 1) * HEAD_DIM)
        y = _rope_lanes(_rmsnorm(q_ref[:, sl], qw), cos, sin, HEAD_DIM // 4)
        qo_ref[:, sl] = y.astype(BF16)
    for h in range(k_ref.shape[1] // HEAD_DIM):
        sl = slice(h * HEAD_DIM, (h + 1) * HEAD_DIM)
        y = _rope_lanes(_rmsnorm(k_ref[:, sl], kw), cos, sin, HEAD_DIM // 4)
        ks_ref[:, sl] = y
        ko_ref[:, sl] = y.astype(BF16)
    vo_ref[...] = v_ref[...].astype(BF16)


def _qk_post(proj, q_norm, k_norm, cos, sin, rows_p, seq_s):
    t = proj.shape[0]
    qw_, kvw = A_HEADS * HEAD_DIM, A_KV_HEADS * HEAD_DIM
    tm = _tile(math.gcd(rows_p, seq_s), POST_ROW_TILE)
    p_tiles, s_tiles = rows_p // tm, seq_s // tm
    kblk = qw_ // kvw

    def pos_tile(i):
        return jnp.where(i >= p_tiles, jnp.maximum(i - p_tiles, 0) % s_tiles, s_tiles)

    vmem = 2 * (tm * (qw_ + 2 * kvw) * 4 + tm * (qw_ + 2 * kvw) * 2 + tm * kvw * 4) + 4 * tm * qw_ * 4 + 4 * MIB
    return pl.pallas_call(
        _qk_post_kernel,
        grid=(t // tm,),
        in_specs=[
            pl.BlockSpec((tm, qw_), lambda i: (i, 0)),
            pl.BlockSpec((tm, kvw), lambda i: (i, kblk)),
            pl.BlockSpec((tm, kvw), lambda i: (i, kblk + 1)),
            pl.BlockSpec((1, HEAD_DIM), lambda i: (0, 0)),
            pl.BlockSpec((1, HEAD_DIM), lambda i: (0, 0)),
            pl.BlockSpec((tm, LANE), lambda i: (pos_tile(i), 0)),
            pl.BlockSpec((tm, LANE), lambda i: (pos_tile(i), 0)),
        ],
        out_specs=[
            pl.BlockSpec((tm, qw_), lambda i: (i, 0)),
            pl.BlockSpec((tm, kvw), lambda i: (i, 0)),
            pl.BlockSpec((tm, kvw), lambda i: (i, 0)),
            pl.BlockSpec((tm, kvw), lambda i: (i, 0)),
        ],
        out_shape=[
            jax.ShapeDtypeStruct((t, qw_), BF16),
            jax.ShapeDtypeStruct((t, kvw), BF16),
            jax.ShapeDtypeStruct((t, kvw), BF16),
            jax.ShapeDtypeStruct((t, kvw), F32),
        ],
        compiler_params=_params(("parallel",), vmem),
        name="even_qk_norm_rope",
    )(proj, proj, proj, q_norm.reshape(1, HEAD_DIM), k_norm.reshape(1, HEAD_DIM), cos, sin)


def _attention_kernel(*refs, group, scale, extra, aliased):
    it = iter(refs)
    q_ref, k_ref, v_ref = next(it), next(it), next(it)
    q2_ref = k2_ref = None
    if extra:
        q2_ref, k2_ref = next(it), next(it)
    if aliased:
        next(it)
    o_ref = next(it)
    k2 = k2_ref[...] if extra else None
    for j in range(k_ref.shape[1] // LANE):
        kv_sl = slice(j * LANE, (j + 1) * LANE)
        k = k_ref[:, kv_sl]
        v = v_ref[:, kv_sl]
        if extra:
            k = jnp.concatenate([k, k2], axis=-1)
        for h in range(group):
            sl = slice((j * group + h) * LANE, (j * group + h + 1) * LANE)
            q = q_ref[:, sl]
            if extra:
                q = jnp.concatenate([q, q2_ref[:, sl]], axis=-1)
            s = lax.dot_general(q, k, (((1,), (1,)), ((), ())), preferred_element_type=F32) * (scale * LOG2_E)
            m = jnp.max(s, axis=-1, keepdims=True)
            p = jnp.exp2(s - m)
            den = jnp.sum(p, axis=-1, keepdims=True)
            o = jnp.dot(p.astype(BF16), v, preferred_element_type=F32) / den
            o_ref[:, sl] = o.astype(o_ref.dtype)


def _attention(q, k, v, *, n_batch, sq, nk, q_row0, k_blk0, k_col0, v_col0, n_kv, group, scale,
               out_cols, kv_per_step=1, q2=None, q2_col0=0, k2=None, k2_blk0=0, prev=None, name):
    t = q.shape[0]
    tq = _tile(math.gcd(sq, q_row0), ATTN_Q_TILE)
    qb0 = q_row0 // tq
    qpb = sq // tq
    kps = kv_per_step
    assert n_kv % kps == 0 and k_col0 % kps == 0 and v_col0 % kps == 0 and q2_col0 % (kps * group) == 0
    kw = kps * LANE
    gw = kps * group * LANE
    kc0, vc0, q2c0 = k_col0 // kps, v_col0 // kps, q2_col0 // (kps * group)
    extra = q2 is not None
    in_specs = [
        pl.BlockSpec((tq, gw), lambda b, g, i: (qb0 + b * qpb + i, g)),
        pl.BlockSpec((nk, kw), lambda b, g, i: (k_blk0 + b, kc0 + g)),
        pl.BlockSpec((nk, kw), lambda b, g, i: (k_blk0 + b, vc0 + g)),
    ]
    args = [q, k, v]
    if extra:
        in_specs += [pl.BlockSpec((tq, gw), lambda b, g, i: (qb0 + b * qpb + i, q2c0 + g)),
                     pl.BlockSpec((nk, LANE), lambda b, g, i: (k2_blk0 + b, 0))]
        args += [q, k2]
    aliases = {}
    if prev is not None:
        in_specs.append(pl.BlockSpec(memory_space=pl.ANY))
        aliases = {len(args): 0}
        args.append(prev)
    kd = 2 * LANE if extra else LANE
    vmem = (2 * (tq * gw * 2 * (2 if extra else 1) + nk * kw * 2 * 2 + nk * LANE * 2 + tq * gw * 2)
            + nk * kd * 2 + 3 * kps * tq * nk * 4 + 4 * MIB)
    return pl.pallas_call(
        functools.partial(_attention_kernel, group=group, scale=scale, extra=extra, aliased=prev is not None),
        grid=(n_batch, n_kv // kps, qpb),
        in_specs=in_specs,
        out_specs=pl.BlockSpec((tq, gw), lambda b, g, i: (qb0 + b * qpb + i, g)),
        out_shape=jax.ShapeDtypeStruct((t, out_cols), BF16),
        input_output_aliases=aliases,
        compiler_params=_params(("parallel", "parallel", "parallel"), vmem),
        name=name,
    )(*args)


def _dft_width_kernel(f_ref, cs_ref, ab_ref):
    r = jnp.dot(f_ref[...].astype(BF16), cs_ref[...], preferred_element_type=F32)
    ab_ref[0] = r[:, :B_GROUP_W].astype(BF16)
    ab_ref[1] = r[:, B_GROUP_W:].astype(BF16)


def _dft_matrix(n):
    base = _tile(n, max(SUBLANE, int(math.sqrt(n))))
    k = jnp.arange(n, dtype=I32)

    def tables(rows):
        ang = ((rows[:, None] * k[None, :]) % n).astype(F32) * (2.0 * math.pi / n)
        return jnp.cos(ang), jnp.sin(ang)

    c1, s1 = tables(jnp.arange(n // base, dtype=I32) * base)
    c2, s2 = tables(jnp.arange(base, dtype=I32))
    q1 = jnp.concatenate([c1, s1], axis=1)[:, None, :]
    q2 = jnp.concatenate([-s1, c1], axis=1)[:, None, :]
    r1 = jnp.concatenate([c2, c2], axis=1)[None, :, :]
    r2 = jnp.concatenate([s2, s2], axis=1)[None, :, :]
    return (q1 * r1 + q2 * r2).reshape(n, 2 * n)


def _dft_tables(n):
    j = jnp.arange(n, dtype=I32)
    jk = (j[:, None] * j[None, :]) % n
    ang = jk.astype(F32) * (2.0 * math.pi / n)
    return jnp.cos(ang), jnp.sin(ang)


def _dft_width(proj, col0_blk, rows_p, seq_s):
    t = proj.shape[0]
    bw = B_GROUPS * B_GROUP_W
    cw, sw = _dft_tables(B_GROUP_W)
    cs = jnp.concatenate([cw, -sw], axis=1).astype(BF16)
    tm = _tile(math.gcd(rows_p, seq_s), 2 * ROW_TILE)
    vmem = 2 * (tm * LANE * 4 + 2 * tm * LANE * 2) + 4 * tm * LANE * 4 + 4 * MIB
    return pl.pallas_call(
        _dft_width_kernel,
        grid=(t // tm, B_GROUPS),
        in_specs=[pl.BlockSpec((tm, B_GROUP_W), lambda i, g: (i, col0_blk + g)),
                  pl.BlockSpec((B_GROUP_W, 2 * B_GROUP_W), lambda i, g: (0, 0))],
        out_specs=pl.BlockSpec((2, tm, B_GROUP_W), lambda i, g: (0, i, g)),
        out_shape=jax.ShapeDtypeStruct((2, t, bw), BF16),
        compiler_params=_params(("parallel", "parallel"), vmem),
        name="fourier_width_dft",
    )(proj, cs)


def _dft_seq(ab, n_batch, seq, row0, prev, name):
    _, t, bw = ab.shape
    f = _dft_matrix(seq).astype(BF16)
    tm = _tile(math.gcd(seq, row0), 2 * ROW_TILE)
    tk = tm
    tn = _tile(bw, 2 * COL_TILE, LANE)
    kt = seq // tk
    rb0 = row0 // tk
    ob0 = row0 // tm
    scale = 1.0 / math.sqrt(seq * B_GROUP_W)
    return _matmul(
        f, ab,
        grid=(n_batch, seq // tm, bw // tn, 2 * kt),
        a_spec=pl.BlockSpec((tm, tk), lambda b, i, j, kk: (i, kk)),
        b_spec=pl.BlockSpec((None, tk, tn), lambda b, i, j, kk: (kk // kt, rb0 + b * kt + kk % kt, j)),
        o_spec=pl.BlockSpec((tm, tn), lambda b, i, j, kk: (ob0 + b * (seq // tm) + i, j)),
        out_shape=jax.ShapeDtypeStruct((t, bw), BF16),
        tile=(tm, tn, tk), scale=scale, prev=prev, name=name)


def _kv_post_kernel(ckv_ref, kr_ref, w_ref, cos_ref, sin_ref, cn_ref, cb_ref, kr_out_ref):
    y = _rmsnorm(ckv_ref[...], w_ref[...])
    cn_ref[...] = y
    cb_ref[...] = y.astype(BF16)
    kr_out_ref[...] = _rope_lanes(kr_ref[...], cos_ref[...], sin_ref[...], QK_ROPE // 4).astype(BF16)


def _kv_post(proj, ckv_blk, kr_blk, kv_norm, cos, sin, rows_p, seq_s):
    t = proj.shape[0]
    tm = _tile(math.gcd(rows_p, seq_s), POST_ROW_TILE)
    p_tiles, s_tiles = rows_p // tm, seq_s // tm

    def pos_tile(i):
        return jnp.where(i >= p_tiles, jnp.maximum(i - p_tiles, 0) % s_tiles, s_tiles)

    vmem = 2 * (tm * KV_LORA * 10 + tm * LANE * 14) + 4 * tm * KV_LORA * 4 + 4 * MIB
    return pl.pallas_call(
        _kv_post_kernel,
        grid=(t // tm,),
        in_specs=[
            pl.BlockSpec((tm, KV_LORA), lambda i: (i, ckv_blk)),
            pl.BlockSpec((tm, LANE), lambda i: (i, kr_blk)),
            pl.BlockSpec((1, KV_LORA), lambda i: (0, 0)),
            pl.BlockSpec((tm, LANE), lambda i: (pos_tile(i), 0)),
            pl.BlockSpec((tm, LANE), lambda i: (pos_tile(i), 0)),
        ],
        out_specs=[
            pl.BlockSpec((tm, KV_LORA), lambda i: (i, 0)),
            pl.BlockSpec((tm, KV_LORA), lambda i: (i, 0)),
            pl.BlockSpec((tm, LANE), lambda i: (i, 0)),
        ],
        out_shape=[
            jax.ShapeDtypeStruct((t, KV_LORA), F32),
            jax.ShapeDtypeStruct((t, KV_LORA), BF16),
            jax.ShapeDtypeStruct((t, LANE), BF16),
        ],
        compiler_params=_params(("parallel",), vmem),
        name="odd_kv_norm_rope",
    )(proj, proj, kv_norm.reshape(1, KV_LORA), cos, sin)


def _conv_kernel(u_ref, gb_ref, gc_ref, up_ref, gcp_ref, un_ref, gcn_ref, w_ref, o_ref, *,
                 tr, rows_p, seq_p, seq_s):
    i = pl.program_id(0)
    start = i * tr
    local = jnp.where(start < rows_p, start % seq_p, jnp.maximum(start - rows_p, 0) % seq_s)
    seq = jnp.where(start < rows_p, seq_p, seq_s)
    has_prev = local > 0
    has_next = local + tr < seq
    z = gc_ref[...] * u_ref[...]
    zp = (gcp_ref[...] * up_ref[...])[SUBLANE - 1:SUBLANE, :]
    zn = (gcn_ref[...] * un_ref[...])[0:1, :]
    zp = jnp.where(has_prev, zp, 0.0)
    zn = jnp.where(has_next, zn, 0.0)
    row = lax.broadcasted_iota(I32, z.shape, 0)
    z_prev = jnp.where(row == 0, zp, pltpu.roll(z, 1, 0))
    z_next = jnp.where(row == tr - 1, zn, pltpu.roll(z, tr - 1, 0))
    w = w_ref[...]
    conv = z_prev * w[0:1, :] + z * w[1:2, :] + z_next * w[2:3, :]
    o_ref[...] = (gb_ref[...] * conv).astype(o_ref.dtype)


def _gated_conv(proj, u_blk, gb_blk, gc_blk, conv_w, rows_p, seq_p, seq_s):
    t = proj.shape[0]
    tr = _tile(math.gcd(seq_p, seq_s), POST_ROW_TILE)
    tc = _tile(D_CH, COL_TILE, LANE)
    halo = tr // SUBLANE
    last = t // SUBLANE - 1
    cpb = D_CH // tc

    def main(blk):
        return pl.BlockSpec((tr, tc), lambda i, j: (i, blk * cpb + j))

    def prev(blk):
        return pl.BlockSpec((SUBLANE, tc), lambda i, j: (jnp.maximum(i * halo - 1, 0), blk * cpb + j))

    def nxt(blk):
        return pl.BlockSpec((SUBLANE, tc), lambda i, j: (jnp.minimum((i + 1) * halo, last), blk * cpb + j))

    w8 = jnp.pad(conv_w, ((0, SUBLANE - CONV_W), (0, 0)))
    vmem = 2 * (3 * tr * tc * 4 + 4 * SUBLANE * tc * 4 + tr * tc * 2) + 8 * tr * tc * 4 + 4 * MIB
    return pl.pallas_call(
        functools.partial(_conv_kernel, tr=tr, rows_p=rows_p, seq_p=seq_p, seq_s=seq_s),
        grid=(t // tr, cpb),
        in_specs=[main(u_blk), main(gb_blk), main(gc_blk), prev(u_blk), prev(gc_blk), nxt(u_blk), nxt(gc_blk),
                  pl.BlockSpec((SUBLANE, tc), lambda i, j: (0, j))],
        out_specs=pl.BlockSpec((tr, tc), lambda i, j: (i, j)),
        out_shape=jax.ShapeDtypeStruct((t, D_CH), BF16),
        compiler_params=_params(("parallel", "parallel"), vmem),
        name="odd_gated_conv",
    )(proj, proj, proj, proj, proj, proj, proj, w8)


def _router_kernel(x_ref, nw_ref, sc_ref, sh_ref, rw_ref, rb_ref,
                   h_ref, idx_ref, wt_ref, rank_ref, cnt_ref, carry_ref):
    @pl.when(pl.program_id(0) == 0)
    def _():
        carry_ref[...] = jnp.zeros_like(carry_ref)

    h = _rmsnorm(x_ref[...], nw_ref[...]) * (1.0 + sc_ref[...]) + sh_ref[...]
    for q in range(h.shape[1] // (2 * LANE)):
        k, lanes = _pair_position(q, h_ref.shape[1])
        lo = h[:, (2 * q) * LANE:(2 * q + 1) * LANE]
        hi = h[:, (2 * q + 1) * LANE:(2 * q + 2) * LANE]
        h_ref[pl.ds(k, h.shape[0], stride=SLAB_ROWS), lanes] = _pack_pair(lo, hi)
    logits = lax.dot_general(rw_ref[...], h, (((1,), (1,)), ((), ())),
                             precision=lax.Precision.HIGHEST, preferred_element_type=F32) + rb_ref[...]
    n_e, tm = logits.shape
    eidx = lax.broadcasted_iota(I32, logits.shape, 0).astype(F32)
    work = logits
    vals, hots = [], []
    for k in range(TOP_K):
        m = jnp.max(work, axis=0, keepdims=True)
        sel = jnp.min(jnp.where(work == m, eidx, float(n_e)), axis=0, keepdims=True)
        hot = eidx == sel
        idx_ref[k:k + 1, :] = sel.astype(I32)
        vals.append(m)
        hots.append(hot)
        work = jnp.where(hot, -jnp.inf, work)
    exps = [jnp.exp(v - vals[0]) for v in vals]
    den = exps[0]
    for e in exps[1:]:
        den = den + e
    for k in range(TOP_K):
        wt_ref[k:k + 1, :] = exps[k] / den
    chosen = hots[0]
    for hot in hots[1:]:
        chosen = jnp.logical_or(chosen, hot)
    chosen = jnp.where(chosen, 1.0, 0.0)
    r = lax.broadcasted_iota(I32, (tm, tm), 0)
    c = lax.broadcasted_iota(I32, (tm, tm), 1)
    upper = jnp.where(r < c, 1.0, 0.0).astype(BF16)
    before = jnp.dot(chosen.astype(BF16), upper, preferred_element_type=F32) + carry_ref[:, 0:1]
    for k in range(TOP_K):
        rk = jnp.sum(jnp.where(hots[k], before, 0.0), axis=0, keepdims=True)
        rank_ref[k:k + 1, :] = rk.astype(I32)
    carry_ref[...] = carry_ref[...] + jnp.sum(chosen, axis=1, keepdims=True)
    cnt_ref[...] = carry_ref[...].astype(I32)


def _router(x, nw, mod, layer, router_w, router_b, rows_p, seq_s):
    t, d = x.shape
    n_e = router_w.shape[1]
    tm = _tile(math.gcd(rows_p, seq_s), ROW_TILE, LANE)
    width = _packed_width(d)

    def grp(i):
        return _group_of_row(i * tm, rows_p, seq_s)

    vmem = 2 * (tm * d * 6 + n_e * d * 4) + 4 * tm * d * 4 + tm * tm * 4 + 4 * MIB
    return pl.pallas_call(
        _router_kernel,
        grid=(t // tm,),
        in_specs=[
            pl.BlockSpec((tm, d), lambda i: (i, 0)),
            pl.BlockSpec((1, d), lambda i: (0, 0)),
            pl.BlockSpec((None, None, None, 1, d), lambda i: (layer, 4, grp(i), 0, 0)),
            pl.BlockSpec((None, None, None, 1, d), lambda i: (layer, 3, grp(i), 0, 0)),
            pl.BlockSpec((n_e, d), lambda i: (0, 0)),
            pl.BlockSpec((n_e, 1), lambda i: (0, 0)),
        ],
        out_specs=[
            pl.BlockSpec((tm * SLAB_ROWS, width), lambda i: (i, 0)),
            pl.BlockSpec((TOP_K, tm), lambda i: (0, i)),
            pl.BlockSpec((TOP_K, tm), lambda i: (0, i)),
            pl.BlockSpec((TOP_K, tm), lambda i: (0, i)),
            pl.BlockSpec((n_e, LANE), lambda i: (0, 0)),
        ],
        out_shape=[
            jax.ShapeDtypeStruct((t * SLAB_ROWS, width), jnp.uint32),
            jax.ShapeDtypeStruct((TOP_K, t), I32),
            jax.ShapeDtypeStruct((TOP_K, t), F32),
            jax.ShapeDtypeStruct((TOP_K, t), I32),
            jax.ShapeDtypeStruct((n_e, LANE), I32),
        ],
        scratch_shapes=[pltpu.VMEM((n_e, LANE), F32)],
        compiler_params=_params(("arbitrary",), vmem),
        name="moe_router",
    )(x, nw.reshape(1, d), mod, mod, router_w.T, router_b.reshape(n_e, 1))


def _slab_copy(src_hbm, row, dst, r, sem):
    return pltpu.make_async_copy(src_hbm.at[pl.ds(pl.multiple_of(row * SLAB_ROWS, SLAB_ROWS), SLAB_ROWS), :],
                                 dst.at[pl.ds(pl.multiple_of(r * SLAB_ROWS, SLAB_ROWS), SLAB_ROWS), :], sem)


def _gather_kernel(nv_ref, pairs_ref, tok_ref, tok_next_ref, h_hbm, o_ref, buf0, buf1, sem):
    i = pl.program_id(0)
    nv = nv_ref[0]
    rows = o_ref.shape[0]
    bufs = (buf0, buf1)
    width = buf0.shape[1]

    def start_tile(tok, slot, n_pairs):
        def body(r2, carry):
            for u in range(2):
                r = 2 * r2 + u
                _slab_copy(h_hbm, tok[0, r], bufs[slot], r, sem.at[slot]).start(priority=u)
            return carry
        lax.fori_loop(0, n_pairs, body, 0)

    @pl.when(i == 0)
    def _():
        buf0[...] = jnp.zeros_like(buf0)
        buf1[...] = jnp.zeros_like(buf1)
        start_tile(tok_ref, 0, pairs_ref[0])

    for par in (0, 1):
        @pl.when(jnp.logical_and(i % 2 == par, i + 1 < nv))
        def _():
            start_tile(tok_next_ref, 1 - par, pairs_ref[jnp.minimum(i + 1, nv - 1)])

        @pl.when(jnp.logical_and(i % 2 == par, i < nv))
        def _():
            def body(r, carry):
                _slab_copy(h_hbm, 0, bufs[par], r, sem.at[par]).wait()
                return carry
            lax.fori_loop(0, 2 * pairs_ref[i], body, 0)
            for q in range(o_ref.shape[1] // (2 * LANE)):
                k, lanes = _pair_position(q, width)
                lo, hi = _unpack_pair(bufs[par][pl.ds(k, rows, stride=SLAB_ROWS), lanes])
                o_ref[:, (2 * q) * LANE:(2 * q + 1) * LANE] = lo.astype(o_ref.dtype)
                o_ref[:, (2 * q + 1) * LANE:(2 * q + 2) * LANE] = hi.astype(o_ref.dtype)

    @pl.when(i >= nv)
    def _():
        o_ref[...] = jnp.zeros_like(o_ref)


def _gather_rows(h_packed, d, token_of, n_valid, tile_pairs, tg):
    nt = token_of.shape[0]
    width = _packed_width(d)
    assert tg % 2 == 0
    vmem = 2 * tg * d * 2 + 2 * tg * d * 2 + 2 * tg * d * 4 + 4 * MIB
    grid_spec = pltpu.PrefetchScalarGridSpec(
        num_scalar_prefetch=2,
        grid=(nt,),
        in_specs=[
            pl.BlockSpec((None, 1, tg), lambda i, nv, tp: (jnp.minimum(i, nv[0] - 1), 0, 0),
                         memory_space=pltpu.SMEM),
            pl.BlockSpec((None, 1, tg), lambda i, nv, tp: (jnp.minimum(i + 1, nv[0] - 1), 0, 0),
                         memory_space=pltpu.SMEM),
            pl.BlockSpec(memory_space=pl.ANY),
        ],
        out_specs=pl.BlockSpec((tg, d), lambda i, nv, tp: (i, 0)),
        scratch_shapes=[pltpu.VMEM((tg * SLAB_ROWS, width), jnp.uint32),
                        pltpu.VMEM((tg * SLAB_ROWS, width), jnp.uint32), pltpu.SemaphoreType.DMA((2,))],
    )
    return pl.pallas_call(
        _gather_kernel,
        grid_spec=grid_spec,
        out_shape=jax.ShapeDtypeStruct((nt * tg, d), BF16),
        compiler_params=_params(("arbitrary",), vmem),
        name="moe_dispatch_gather",
    )(n_valid, tile_pairs, token_of, token_of, h_packed)


def _gate_up_kernel(te_ref, nv_ref, x_ref, wg_ref, wu_ref, bg_ref, bu_ref, o_ref, wg_bf, wu_bf):
    r = pl.program_id(1)
    e = te_ref[r]
    e_prev = te_ref[jnp.maximum(r - 1, 0)]

    @pl.when(jnp.logical_or(r == 0, e != e_prev))
    def _():
        wg_bf[...] = wg_ref[...].astype(BF16)
        wu_bf[...] = wu_ref[...].astype(BF16)

    @pl.when(r < nv_ref[0])
    def _():
        x = x_ref[...]
        g = jnp.dot(x, wg_bf[...], preferred_element_type=F32) + bg_ref[...]
        u = jnp.dot(x, wu_bf[...], preferred_element_type=F32) + bu_ref[...]
        g = jnp.minimum(g, SWIGLU_LIMIT)
        u = jnp.clip(u, -SWIGLU_LIMIT, SWIGLU_LIMIT)
        o_ref[...] = ((u + 1.0) * g * jax.nn.sigmoid(SWIGLU_ALPHA * g)).astype(o_ref.dtype)

    @pl.when(r >= nv_ref[0])
    def _():
        o_ref[...] = jnp.zeros_like(o_ref)


def _gate_up(xs, tile_expert, n_valid, layer, w_gate, w_up, b_gate, b_up, tm):
    p, d = xs.shape
    depth, n_e, _, ff = w_gate.shape
    nt = p // tm
    nc = _tile(ff, GATE_UP_COL_TILE, LANE)

    def row(c, r, te, nv):
        return jnp.minimum(r, nv[0] - 1)

    vmem = 2 * (tm * d * 2 + 2 * d * nc * 4 + tm * nc * 2) + 2 * d * nc * 2 + 6 * tm * nc * 4 + 4 * MIB
    grid_spec = pltpu.PrefetchScalarGridSpec(
        num_scalar_prefetch=2,
        grid=(ff // nc, nt),
        in_specs=[
            pl.BlockSpec((tm, d), lambda c, r, te, nv: (row(c, r, te, nv), 0)),
            pl.BlockSpec((None, None, d, nc), lambda c, r, te, nv: (layer, te[r], 0, c)),
            pl.BlockSpec((None, None, d, nc), lambda c, r, te, nv: (layer, te[r], 0, c)),
            pl.BlockSpec((None, None, 1, nc), lambda c, r, te, nv: (layer, te[r], 0, c)),
            pl.BlockSpec((None, None, 1, nc), lambda c, r, te, nv: (layer, te[r], 0, c)),
        ],
        out_specs=pl.BlockSpec((tm, nc), lambda c, r, te, nv: (r, c)),
        scratch_shapes=[pltpu.VMEM((d, nc), BF16), pltpu.VMEM((d, nc), BF16)],
    )
    return pl.pallas_call(
        _gate_up_kernel,
        grid_spec=grid_spec,
        out_shape=jax.ShapeDtypeStruct((p, ff), BF16),
        compiler_params=_params(("arbitrary", "arbitrary"), vmem),
        name="moe_gate_up",
    )(tile_expert, n_valid, xs, w_gate, w_up, b_gate.reshape(depth, n_e, 1, ff), b_up.reshape(depth, n_e, 1, ff))


def _down_kernel(te_ref, nv_ref, a_ref, wd_ref, bd_ref, o_ref, wd_bf):
    r = pl.program_id(0)
    e = te_ref[r]
    e_prev = te_ref[jnp.maximum(r - 1, 0)]
    rows = a_ref.shape[0]

    @pl.when(jnp.logical_or(r == 0, e != e_prev))
    def _():
        wd_bf[...] = wd_ref[...].astype(BF16)

    @pl.when(r < nv_ref[0])
    def _():
        a = a_ref[...]
        for q in range(wd_bf.shape[1] // (2 * LANE)):
            sl = slice(2 * q * LANE, (2 * q + 2) * LANE)
            y = jnp.dot(a, wd_bf[:, sl], preferred_element_type=F32) + bd_ref[:, sl]
            k, lanes = _pair_position(q, o_ref.shape[1])
            o_ref[pl.ds(k, rows, stride=SLAB_ROWS), lanes] = _pack_pair(y[:, :LANE], y[:, LANE:])

    @pl.when(r >= nv_ref[0])
    def _():
        o_ref[...] = jnp.zeros_like(o_ref)


def _down(act, tile_expert, n_valid, layer, w_down, b_down, tm):
    p, ff = act.shape
    depth, n_e, _, d = w_down.shape
    nt = p // tm
    width = _packed_width(d)
    vmem = 2 * (tm * ff * 2 + tm * d * 2 + ff * d * 4) + ff * d * 2 + 4 * tm * 2 * LANE * 4 + 4 * MIB
    grid_spec = pltpu.PrefetchScalarGridSpec(
        num_scalar_prefetch=2,
        grid=(nt,),
        in_specs=[
            pl.BlockSpec((tm, ff), lambda r, te, nv: (jnp.minimum(r, nv[0] - 1), 0)),
            pl.BlockSpec((None, None, ff, d), lambda r, te, nv: (layer, te[r], 0, 0)),
            pl.BlockSpec((None, None, 1, d), lambda r, te, nv: (layer, te[r], 0, 0)),
        ],
        out_specs=pl.BlockSpec((tm * SLAB_ROWS, width), lambda r, te, nv: (r, 0)),
        scratch_shapes=[pltpu.VMEM((ff, d), BF16)],
    )
    return pl.pallas_call(
        _down_kernel,
        grid_spec=grid_spec,
        out_shape=jax.ShapeDtypeStruct((p * SLAB_ROWS, width), jnp.uint32),
        compiler_params=_params(("arbitrary",), vmem),
        name="moe_down",
    )(tile_expert, n_valid, act, w_down, b_down.reshape(depth, n_e, 1, d))


def _combine_kernel(pos_ref, pos_next_ref, x_ref, wt_ref, g_ref, fw_ref, y_hbm, *rest, p_tiles):
    if p_tiles is None:
        o_ref, buf0, buf1, sem = rest
    else:
        oc_ref, ol_ref, buf0, buf1, sem, o_ref = rest
    i = pl.program_id(0)
    n = pl.num_programs(0)
    rows = x_ref.shape[0]
    bufs = (buf0, buf1)
    width = buf0.shape[1]

    def start_tile(pos, slot):
        def body(r, carry):
            for k in range(TOP_K):
                _slab_copy(y_hbm, pos[k, r], bufs[slot], k * rows + r, sem.at[slot]).start(priority=k % 2)
            return carry
        lax.fori_loop(0, rows, body, 0, unroll=4)

    @pl.when(i == 0)
    def _():
        start_tile(pos_ref, 0)

    for par in (0, 1):
        @pl.when(jnp.logical_and(i % 2 == par, i + 1 < n))
        def _():
            start_tile(pos_next_ref, 1 - par)

        @pl.when(i % 2 == par)
        def _():
            def wait_body(r, carry):
                for k in range(TOP_K):
                    _slab_copy(y_hbm, 0, bufs[par], k * rows + r, sem.at[par]).wait()
                return carry
            lax.fori_loop(0, rows, wait_body, 0, unroll=4)
            wt = wt_ref[...]
            wb = [jnp.broadcast_to(wt[:, k:k + 1], (rows, LANE)) for k in range(TOP_K)]
            for c in range(x_ref.shape[1] // (2 * LANE)):
                row, lanes = _pair_position(c, width)
                acc_lo = acc_hi = None
                for k in range(TOP_K):
                    lo, hi = _unpack_pair(
                        bufs[par][pl.ds(k * rows * SLAB_ROWS + row, rows, stride=SLAB_ROWS), lanes])
                    acc_lo = wb[k] * lo if acc_lo is None else acc_lo + wb[k] * lo
                    acc_hi = wb[k] * hi if acc_hi is None else acc_hi + wb[k] * hi
                s_lo = slice((2 * c) * LANE, (2 * c + 1) * LANE)
                s_hi = slice((2 * c + 1) * LANE, (2 * c + 2) * LANE)
                o_ref[:, s_lo] = x_ref[:, s_lo] + g_ref[:, s_lo] * acc_lo
                o_ref[:, s_hi] = x_ref[:, s_hi] + g_ref[:, s_hi] * acc_hi

    if p_tiles is not None:
        @pl.when(i < p_tiles)
        def _():
            oc_ref[...] = _rmsnorm(o_ref[...], fw_ref[...])

        @pl.when(i >= p_tiles)
        def _():
            ol_ref[...] = _rmsnorm(o_ref[...], fw_ref[...])


def _combine(x, y_pairs, pos, wts, mod, layer, final_w, rows_p, seq_s, *, final_norm):
    t, d = x.shape
    tc = _tile(math.gcd(rows_p, seq_s), COMBINE_ROW_TILE)
    nt = t // tc
    pos3 = pos.reshape(TOP_K, nt, tc).transpose(1, 0, 2)

    def grp(i):
        return _group_of_row(i * tc, rows_p, seq_s)

    width = _packed_width(d)
    vmem = 2 * TOP_K * tc * d * 2 + 2 * (3 * tc * d * 4 + tc * LANE * 4) + 4 * tc * d * 4 + 4 * MIB
    scratch = [pltpu.VMEM((TOP_K * tc * SLAB_ROWS, width), jnp.uint32),
               pltpu.VMEM((TOP_K * tc * SLAB_ROWS, width), jnp.uint32), pltpu.SemaphoreType.DMA((2,))]
    if final_norm:
        p_tiles = rows_p // tc
        out_specs = [pl.BlockSpec((tc, d), lambda i: (jnp.minimum(i, p_tiles - 1), 0)),
                     pl.BlockSpec((tc, d), lambda i: (jnp.maximum(i - p_tiles, 0), 0))]
        out_shape = [jax.ShapeDtypeStruct((rows_p, d), F32), jax.ShapeDtypeStruct((t - rows_p, d), F32)]
        scratch.append(pltpu.VMEM((tc, d), F32))
    else:
        p_tiles = None
        out_specs = pl.BlockSpec((tc, d), lambda i: (i, 0))
        out_shape = jax.ShapeDtypeStruct((t, d), F32)
    return pl.pallas_call(
        functools.partial(_combine_kernel, p_tiles=p_tiles),
        grid=(nt,),
        in_specs=[
            pl.BlockSpec((None, TOP_K, tc), lambda i: (i, 0, 0), memory_space=pltpu.SMEM),
            pl.BlockSpec((None, TOP_K, tc), lambda i: (jnp.minimum(i + 1, nt - 1), 0, 0), memory_space=pltpu.SMEM),
            pl.BlockSpec((tc, d), lambda i: (i, 0)),
            pl.BlockSpec((tc, TOP_K), lambda i: (i, 0)),
            pl.BlockSpec((None, None, None, 1, d), lambda i: (layer, 5, grp(i), 0, 0)),
            pl.BlockSpec((1, d), lambda i: (0, 0)),
            pl.BlockSpec(memory_space=pl.ANY),
        ],
        out_specs=out_specs,
        out_shape=out_shape,
        scratch_shapes=scratch,
        compiler_params=_params(("arbitrary",), vmem),
        name="moe_combine",
    )(pos3, pos3, x, wts.T, mod, final_w.reshape(1, d), y_pairs)


def _moe_layer(x, nw, mod, layer, router_w, router_b, w_gate, b_gate, w_up, b_up, w_down, b_down,
               final_w, rows_p, seq_s, *, final_norm):
    t, d = x.shape
    n_e = router_w.shape[1]
    tm = EXPERT_ROW_TILE
    tg = _tile(tm, GATHER_ROW_TILE)
    h, idx, wts, rank, cnt = _router(x, nw, mod, layer, router_w, router_b, rows_p, seq_s)
    counts = cnt[:, 0]
    tiles_e = (counts + tm - 1) // tm
    e_ids = jnp.arange(n_e, dtype=I32)
    tile_end = jnp.sum(jnp.where(e_ids[None, :] <= e_ids[:, None], tiles_e[None, :], 0), axis=1)
    offs = (tile_end - tiles_e) * tm
    pos = jnp.sum(jnp.where(idx[:, :, None] == e_ids[None, None, :], offs[None, None, :], 0), axis=-1) + rank
    n_tiles = (t * TOP_K + tm - 1) // tm + n_e
    p_rows = n_tiles * tm
    n_valid_tiles = tile_end[-1]
    tok = jnp.broadcast_to(jnp.arange(t, dtype=I32)[None, :], (TOP_K, t))
    token_of = jnp.zeros((p_rows,), I32).at[pos.reshape(-1)].set(tok.reshape(-1))
    tile_ids = jnp.minimum(jnp.arange(n_tiles, dtype=I32), n_valid_tiles - 1)
    tile_expert = jnp.sum((tile_end[None, :] <= tile_ids[:, None]).astype(I32), axis=1)
    nv_m = n_valid_tiles.reshape(1).astype(I32)
    nv_g = (n_valid_tiles * (tm // tg)).reshape(1).astype(I32)
    g_start = jnp.arange(p_rows // tg, dtype=I32) * tg
    g_expert = jnp.repeat(tile_expert, tm // tg)
    hot = g_expert[:, None] == e_ids[None, :]
    g_end = jnp.sum(jnp.where(hot, (offs + counts)[None, :], 0), axis=1)
    g_rows = jnp.clip(g_end - g_start, 0, tg)
    tile_pairs = ((g_rows + 1) // 2).astype(I32)
    xs = _gather_rows(h, d, token_of.reshape(p_rows // tg, 1, tg), nv_g, tile_pairs, tg)
    act = _gate_up(xs, tile_expert, nv_m, layer, w_gate, w_up, b_gate, b_up, tm)
    y_pairs = _down(act, tile_expert, nv_m, layer, w_down, b_down, tm)
    return _combine(x, y_pairs, pos, wts, mod, layer, final_w, rows_p, seq_s, final_norm=final_norm)


def _even_mixer(x, mod, layer, nw, w_in, q_norm, k_norm, w_out, cache_k, cache_v, dims):
    rows_p, seq_p, n_bp, seq_s, n_bs = dims
    t = x.shape[0]
    qw_, kvw = A_HEADS * HEAD_DIM, A_KV_HEADS * HEAD_DIM
    past = cache_k.shape[1]
    nk_s = past + seq_s
    proj = _norm_matmul(x, 0, nw, w_in.astype(BF16), rows_p, seq_s, mod=mod, layer=layer,
                        m_shift=0, m_scale=1, name="even_in_proj")
    tm_post = _tile(math.gcd(rows_p, seq_s), POST_ROW_TILE)
    cos, sin = _rope_tables(seq_s, HEAD_DIM, tm_post)
    qh, kh, vh, kst = _qk_post(proj, q_norm, k_norm, cos, sin, rows_p, seq_s)
    scale = HEAD_DIM ** -0.5
    group = A_HEADS // A_KV_HEADS
    attn = _attention(qh, kh, vh, n_batch=n_bp, sq=seq_p, nk=seq_p, q_row0=0, k_blk0=0, k_col0=0, v_col0=0,
                      n_kv=A_KV_HEADS, group=group, scale=scale, out_cols=qw_, prev=jnp.zeros((t, qw_), BF16),
                      kv_per_step=_tile(A_KV_HEADS, CTX_KV_HEADS_PER_STEP, 1), name="even_attention_ctx")
    k_lat = kh[rows_p:].reshape(n_bs, seq_s, kvw)
    v_lat = vh[rows_p:].reshape(n_bs, seq_s, kvw)
    k_all = jnp.concatenate([cache_k.reshape(n_bs, past, kvw).astype(BF16), k_lat], axis=1).reshape(n_bs * nk_s, kvw)
    v_all = jnp.concatenate([cache_v.reshape(n_bs, past, kvw).astype(BF16), v_lat], axis=1).reshape(n_bs * nk_s, kvw)
    attn = _attention(qh, k_all, v_all, n_batch=n_bs, sq=seq_s, nk=nk_s, q_row0=rows_p, k_blk0=0, k_col0=0,
                      v_col0=0, n_kv=A_KV_HEADS, group=group, scale=scale, out_cols=qw_, prev=attn,
                      kv_per_step=_tile(A_KV_HEADS, LATENT_GQA_KV_HEADS_PER_STEP, 1), name="even_attention_latent")
    ab = _dft_width(proj, (qw_ + 2 * kvw) // B_GROUP_W, rows_p, seq_s)
    four = _dft_seq(ab, n_bp, seq_p, 0, jnp.zeros((t, B_GROUPS * B_GROUP_W), BF16), "fourier_seq_dft_ctx")
    four = _dft_seq(ab, n_bs, seq_s, rows_p, four, "fourier_seq_dft_latent")
    w_out_bf = w_out.astype(BF16)
    x_new = _out_proj(attn, four, w_out_bf[:qw_], w_out_bf[qw_:], x, mod, layer, 2, rows_p, seq_s,
                      name="even_out_proj")
    k_state = kst[:rows_p].reshape(n_bp, seq_p, A_KV_HEADS, HEAD_DIM)
    v_state = proj[:rows_p, qw_ + kvw:qw_ + 2 * kvw].reshape(n_bp, seq_p, A_KV_HEADS, HEAD_DIM)
    return x_new, k_state, v_state


def _odd_mixer(x, mod, layer, nw, w_in, q_a_norm, kv_a_norm, w_uq, w_ukv, conv_w, w_out,
               cache_ckv, cache_kr, dims):
    rows_p, seq_p, n_bp, seq_s, n_bs = dims
    d = x.shape[1]
    past = cache_ckv.shape[1]
    nk_s = past + seq_s
    o1, o2, o3 = Q_LORA, Q_LORA + KV_LORA, Q_LORA + KV_LORA + QK_ROPE
    o4, o5 = o3 + D_CH, o3 + 2 * D_CH
    pad_w = LANE - QK_ROPE
    w_in_r = jnp.concatenate([w_in[:, o3:o4], w_in[:, o4:o5], w_in[:, o5:], w_in[:, o1:o2], w_in[:, :o1],
                              w_in[:, o2:o3], jnp.zeros((d, pad_w), w_in.dtype)], axis=1).astype(BF16)
    c_ckv = 3 * D_CH
    c_cq = c_ckv + KV_LORA
    c_kr = c_cq + Q_LORA
    assert c_ckv % KV_LORA == 0 and c_cq % Q_LORA == 0 and c_kr % LANE == 0
    proj = _norm_matmul(x, 0, nw, w_in_r, rows_p, seq_s, mod=mod, layer=layer, m_shift=0, m_scale=1,
                        name="odd_in_proj")
    hq = QK_NOPE + QK_ROPE
    wq = w_uq.reshape(Q_LORA, C_HEADS, hq)
    wq_rope = jnp.pad(wq[:, :, QK_NOPE:], ((0, 0), (0, 0), (0, pad_w)))
    wq_r = jnp.concatenate([wq[:, :, :QK_NOPE].reshape(Q_LORA, C_HEADS * QK_NOPE),
                            wq_rope.reshape(Q_LORA, C_HEADS * LANE)], axis=1).astype(BF16)
    tm = _tile(math.gcd(rows_p, seq_s), ROW_TILE)
    tn = _tile(wq_r.shape[1], IN_PROJ_COL_TILE, LANE)
    assert (C_HEADS * QK_NOPE) % tn == 0
    cos_q, sin_q = _rope_tables(seq_s, QK_ROPE, tm)
    q = _norm_matmul(proj, c_cq // Q_LORA, q_a_norm, wq_r, rows_p, seq_s,
                     rope=(cos_q, sin_q, QK_ROPE // 4, (C_HEADS * QK_NOPE) // tn), out_dtype=BF16,
                     name="odd_q_up_proj")
    tm_post = _tile(math.gcd(rows_p, seq_s), POST_ROW_TILE)
    cos_k, sin_k = _rope_tables(seq_s, QK_ROPE, tm_post)
    ckvn, ckvn_bf, kr = _kv_post(proj, c_ckv // KV_LORA, c_kr // LANE, kv_a_norm, cos_k, sin_k, rows_p, seq_s)
    hkv = QK_NOPE + V_HEAD
    wkv = w_ukv.reshape(KV_LORA, C_HEADS, hkv)
    wkv_r = jnp.concatenate([wkv[:, :, :QK_NOPE].reshape(KV_LORA, C_HEADS * QK_NOPE),
                             wkv[:, :, QK_NOPE:].reshape(KV_LORA, C_HEADS * V_HEAD)], axis=1).astype(BF16)
    n_kv_cols = wkv_r.shape[1]
    ckv_all = jnp.concatenate([cache_ckv.astype(BF16), ckvn_bf[rows_p:].reshape(n_bs, seq_s, KV_LORA)],
                              axis=1).reshape(n_bs * nk_s, KV_LORA)
    kr_cache = jnp.pad(cache_kr, ((0, 0), (0, 0), (0, pad_w))).astype(BF16)
    kr_all = jnp.concatenate([kr_cache, kr[rows_p:].reshape(n_bs, seq_s, LANE)], axis=1).reshape(n_bs * nk_s, LANE)

    def kv_up(lat, name):
        m = lat.shape[0]
        tmm = _tile(m, ROW_TILE)
        tnn = _tile(n_kv_cols, 2 * COL_TILE, LANE)
        return _matmul(lat, wkv_r, grid=(1, m // tmm, n_kv_cols // tnn, 1),
                       a_spec=pl.BlockSpec((tmm, KV_LORA), lambda b, i, j, kk: (i, 0)),
                       b_spec=pl.BlockSpec((KV_LORA, tnn), lambda b, i, j, kk: (0, j)),
                       o_spec=pl.BlockSpec((tmm, tnn), lambda b, i, j, kk: (i, j)),
                       out_shape=jax.ShapeDtypeStruct((m, n_kv_cols), BF16),
                       tile=(tmm, tnn, KV_LORA), name=name)

    kv_p = kv_up(ckvn_bf[:rows_p], "odd_kv_up_proj_ctx")
    kv_s = kv_up(ckv_all, "odd_kv_up_proj_latent")
    scale = hq ** -0.5
    out_cols = C_HEADS * V_HEAD
    attn = _attention(q, kv_p, kv_p, n_batch=n_bp, sq=seq_p, nk=seq_p, q_row0=0, k_blk0=0, k_col0=0,
                      v_col0=C_HEADS, n_kv=C_HEADS, group=1, scale=scale, out_cols=out_cols,
                      q2=q, q2_col0=C_HEADS, k2=kr, k2_blk0=0, prev=jnp.zeros((x.shape[0], out_cols), BF16),
                      kv_per_step=_tile(C_HEADS, CTX_KV_HEADS_PER_STEP, 1), name="odd_attention_ctx")
    attn = _attention(q, kv_s, kv_s, n_batch=n_bs, sq=seq_s, nk=nk_s, q_row0=rows_p, k_blk0=0, k_col0=0,
                      v_col0=C_HEADS, n_kv=C_HEADS, group=1, scale=scale, out_cols=out_cols,
                      q2=q, q2_col0=C_HEADS, k2=kr_all, k2_blk0=0, prev=attn,
                      kv_per_step=_tile(C_HEADS, LATENT_MLA_HEADS_PER_STEP, 1), name="odd_attention_latent")
    conv = _gated_conv(proj, 0, 1, 2, conv_w, rows_p, seq_p, seq_s)
    w_out_bf = w_out.astype(BF16)
    x_new = _out_proj(attn, conv, w_out_bf[:out_cols], w_out_bf[out_cols:], x, mod, layer, 2, rows_p, seq_s,
                      name="odd_out_proj")
    ckv_state = ckvn[:rows_p].reshape(n_bp, seq_p, KV_LORA)
    kr_state = proj[:rows_p, c_kr:c_kr + QK_ROPE].reshape(n_bp, seq_p, QK_ROPE)
    return x_new, ckv_state, kr_state


def kernel(x_prompt, x_sample, cache_attn_k, cache_attn_v, cache_mla_ckv, cache_mla_krope, c, c_ctx,
           norm1_w, norm2_w, w_ada, b_ada,
           even_w_in, even_q_norm, even_k_norm, even_w_out,
           odd_w_in, odd_q_a_norm, odd_kv_a_norm, odd_w_uq, odd_w_ukv, odd_conv_w, odd_w_out,
           router_w, router_b, w_gate, b_gate, w_up, b_up, w_down, b_down, final_norm_w):
    n_bp, seq_p, d = x_prompt.shape
    n_bs, seq_s, _ = x_sample.shape
    depth = w_ada.shape[0]
    rows_p = n_bp * seq_p
    dims = (rows_p, seq_p, n_bp, seq_s, n_bs)
    n_groups = 1 + n_bs
    g8 = -(-n_groups // SUBLANE) * SUBLANE
    cond = jnp.concatenate([c_ctx[None, :], c, jnp.zeros((g8 - n_groups, d), F32)], axis=0)
    ada = _ada_modulation(cond, w_ada, b_ada)
    mod = ada[:, :n_groups].reshape(depth, n_groups, N_MOD, d).transpose(0, 2, 1, 3)[:, :, :, None, :]
    x = jnp.concatenate([x_prompt.reshape(rows_p, d), x_sample.reshape(n_bs * seq_s, d)], axis=0)
    st_k, st_v, st_ckv, st_kr = [], [], [], []
    for l in range(depth):
        j = l // 2
        if l % 2 == 0:
            x, ks, vs = _even_mixer(x, mod, l, norm1_w[l], even_w_in[j], even_q_norm[j], even_k_norm[j],
                                    even_w_out[j], cache_attn_k[:, j], cache_attn_v[:, j], dims)
            st_k.append(ks)
            st_v.append(vs)
        else:
            x, cs, rs = _odd_mixer(x, mod, l, norm1_w[l], odd_w_in[j], odd_q_a_norm[j], odd_kv_a_norm[j],
                                   odd_w_uq[j], odd_w_ukv[j], odd_conv_w[j], odd_w_out[j],
                                   cache_mla_ckv[:, j], cache_mla_krope[:, j], dims)
            st_ckv.append(cs)
            st_kr.append(rs)
        x = _moe_layer(x, norm2_w[l], mod, l, router_w[l], router_b[l], w_gate, b_gate, w_up, b_up,
                       w_down, b_down, final_norm_w, rows_p, seq_s, final_norm=(l == depth - 1))
    y_ctx, y_lat = x
    y_prompt = y_ctx.reshape(n_bp, seq_p, d)
    y_sample = y_lat.reshape(n_bs, seq_s, d)
    return (y_prompt, y_sample, jnp.stack(st_k, axis=1), jnp.stack(st_v, axis=1),
            jnp.stack(st_ckv, axis=1), jnp.stack(st_kr, axis=1))
```

```python
import functools
import math

import jax
import jax.numpy as jnp
from jax import lax
from jax.experimental import pallas as pl
from jax.experimental.pallas import tpu as pltpu

F32 = jnp.float32
BF16 = jnp.bfloat16
I32 = jnp.int32

GRID_W = 64
HEAD_DIM = 128
ROPE_THETA = 10000.0
NORM_EPS = 1e-6
A_HEADS = 24
A_KV_HEADS = 6
B_GROUPS = 8
B_GROUP_W = 128
C_HEADS = 24
Q_LORA = 896
KV_LORA = 512
QK_NOPE = 128
QK_ROPE = 64
V_HEAD = 128
D_CH = 1024
CONV_W = 3
N_EXPERTS = 32
TOP_K = 4
EXPERT_FF = 1024
SWIGLU_LIMIT = 7.0
SWIGLU_ALPHA = 1.702
N_MOD = 6
LOG2_E = math.log2(math.e)

LANE = 128
SUBLANE = 8
D_MODEL = 4096
SLAB_ROWS = D_MODEL // (2 * LANE)
V7X_VMEM_BYTES = 64 * 2**20
MIB = 2**20

ROW_TILE = 512
COL_TILE = 512
IN_PROJ_ROW_TILE = 1024
IN_PROJ_COL_TILE = 512
NORM_ROW_CHUNK = 64
POST_ROW_TILE = 256
ATTN_Q_TILE = 256
CTX_KV_HEADS_PER_STEP = 8
LATENT_MLA_HEADS_PER_STEP = 6
LATENT_GQA_KV_HEADS_PER_STEP = 2
ADA_COL_TILE = 512
EXPERT_ROW_TILE = 512
GATE_UP_COL_TILE = 512
GATHER_ROW_TILE = 256
COMBINE_ROW_TILE = 128


def _tile(n, pref, align=SUBLANE):
    t = min(pref, n)
    t -= t % align
    while t > align and n % t:
        t -= align
    assert t >= align and n % t == 0, (n, pref, align)
    return t


def _params(sem, vmem_bytes):
    limit = int(min(max(vmem_bytes, 16 * MIB), V7X_VMEM_BYTES - 6 * MIB))
    return pltpu.CompilerParams(dimension_semantics=sem, vmem_limit_bytes=limit)


def _group_of_row(start, rows_p, seq_s):
    return jnp.where(start < rows_p, 0, 1 + jnp.maximum(start - rows_p, 0) // seq_s)


def _ada_kernel(c_ref, w_ref, b_ref, o_ref):
    c = c_ref[...]
    s = (c * jax.nn.sigmoid(c)).astype(BF16)
    o_ref[...] = jnp.dot(s, w_ref[...].astype(BF16), preferred_element_type=F32) + b_ref[...]


def _ada_modulation(cond, w_ada, b_ada):
    depth, d, n = w_ada.shape
    g8 = cond.shape[0]
    tn = _tile(n, ADA_COL_TILE, LANE)
    vmem = 2 * (d * tn * 4) + d * tn * 2 + 4 * MIB
    return pl.pallas_call(
        _ada_kernel,
        grid=(depth, n // tn),
        in_specs=[
            pl.BlockSpec((g8, d), lambda l, j: (0, 0)),
            pl.BlockSpec((None, d, tn), lambda l, j: (l, 0, j)),
            pl.BlockSpec((None, 1, tn), lambda l, j: (l, 0, j)),
        ],
        out_specs=pl.BlockSpec((None, g8, tn), lambda l, j: (l, 0, j)),
        out_shape=jax.ShapeDtypeStruct((depth, g8, n), F32),
        compiler_params=_params(("parallel", "parallel"), vmem),
        name="ada_modulation",
    )(cond, w_ada, b_ada.reshape(depth, 1, n))


def _rmsnorm(x, w):
    return x * lax.rsqrt(jnp.mean(x * x, axis=-1, keepdims=True) + NORM_EPS) * w


def _pack_pair(lo, hi):
    lo_b = lax.bitcast_convert_type(lo.astype(jnp.bfloat16).astype(F32), jnp.uint32)
    hi_b = lax.bitcast_convert_type(hi.astype(jnp.bfloat16).astype(F32), jnp.uint32)
    return hi_b | (lo_b >> 16)


def _packed_width(d):
    assert d == 2 * LANE * SLAB_ROWS, (d, SLAB_ROWS)
    return LANE


def _pair_position(q, width):
    assert width == LANE
    return q, slice(0, LANE)


def _unpack_pair(w):
    lo = lax.bitcast_convert_type(w << 16, F32)
    hi = lax.bitcast_convert_type(w & jnp.uint32(0xFFFF0000), F32)
    return lo, hi


def _rope_lanes(y, cos, sin, half):
    lane = lax.broadcasted_iota(I32, y.shape, 1)
    partner = jnp.where(lane % (2 * half) < half,
                        pltpu.roll(y, LANE - half, 1),
                        pltpu.roll(y, half, 1))
    return y * cos + partner * sin


def _rope_tables(n_tokens, rot_dim, ident_rows):
    rows = n_tokens // GRID_W
    row = jnp.repeat(jnp.arange(rows), GRID_W).astype(F32)
    col = jnp.tile(jnp.arange(GRID_W), rows).astype(F32)
    half = rot_dim // 2
    inv = ROPE_THETA ** (-jnp.arange(0, half, 2, dtype=F32) / half)
    ang_r = row[:, None] * inv[None, :]
    ang_c = col[:, None] * inv[None, :]
    ang = jnp.concatenate([ang_r, ang_r, ang_c, ang_c], axis=-1)
    sign = jnp.tile(jnp.concatenate([-jnp.ones((half // 2,), F32), jnp.ones((half // 2,), F32)]), 2)
    cos = jnp.cos(ang)
    sin = jnp.sin(ang) * sign[None, :]
    pad = LANE - rot_dim
    cos = jnp.pad(cos, ((0, ident_rows), (0, pad)), constant_values=1.0)
    sin = jnp.pad(sin, ((0, ident_rows), (0, pad)), constant_values=0.0)
    return cos, sin


def _norm_matmul_kernel(*refs, modulate, rope_half):
    it = iter(refs)
    x_ref, nw_ref = next(it), next(it)
    sc_ref = sh_ref = cos_ref = sin_ref = None
    if modulate:
        sc_ref, sh_ref = next(it), next(it)
    w_ref = next(it)
    if rope_half:
        cos_ref, sin_ref = next(it), next(it)
    o_ref, h_ref = next(it), next(it)

    @pl.when(pl.program_id(1) == 0)
    def _():
        rc = _tile(x_ref.shape[0], NORM_ROW_CHUNK, 2 * SUBLANE)

        def chunk(r, carry):
            rows = pl.ds(pl.multiple_of(r * rc, rc), rc)
            y = _rmsnorm(x_ref[rows, :], nw_ref[...])
            if modulate:
                y = y * (1.0 + sc_ref[...]) + sh_ref[...]
            h_ref[rows, :] = y.astype(BF16)
            return carry
        lax.fori_loop(0, x_ref.shape[0] // rc, chunk, 0)

    acc = jnp.dot(h_ref[...], w_ref[...], preferred_element_type=F32)
    if rope_half:
        cos, sin = cos_ref[...], sin_ref[...]
        for c in range(acc.shape[1] // LANE):
            sl = slice(c * LANE, (c + 1) * LANE)
            o_ref[:, sl] = _rope_lanes(acc[:, sl], cos, sin, rope_half).astype(o_ref.dtype)
    else:
        o_ref[...] = acc.astype(o_ref.dtype)


def _norm_matmul(x, x_col, nw, w, rows_p, seq_s, *, mod=None, layer=0, m_shift=0, m_scale=0,
                 rope=None, out_dtype=F32, name):
    t = x.shape[0]
    k, n = w.shape
    tm = _tile(math.gcd(rows_p, seq_s), IN_PROJ_ROW_TILE)
    tn = _tile(n, IN_PROJ_COL_TILE, LANE)
    p_tiles = rows_p // tm
    s_tiles = seq_s // tm

    def grp(i):
        return _group_of_row(i * tm, rows_p, seq_s)

    in_specs = [pl.BlockSpec((tm, k), lambda i, j: (i, x_col)),
                pl.BlockSpec((1, k), lambda i, j: (0, 0))]
    args = [x, nw.reshape(1, k)]
    if mod is not None:
        in_specs += [pl.BlockSpec((None, None, None, 1, k), lambda i, j: (layer, m_scale, grp(i), 0, 0)),
                     pl.BlockSpec((None, None, None, 1, k), lambda i, j: (layer, m_shift, grp(i), 0, 0))]
        args += [mod, mod]
    in_specs.append(pl.BlockSpec((k, tn), lambda i, j: (0, j)))
    args.append(w)
    rope_half = 0
    if rope is not None:
        cos, sin, rope_half, first_tile = rope

        def pos_tile(i, j):
            is_pos = jnp.logical_and(i >= p_tiles, j >= first_tile)
            return jnp.where(is_pos, jnp.maximum(i - p_tiles, 0) % s_tiles, s_tiles)

        in_specs += [pl.BlockSpec((tm, LANE), lambda i, j: (pos_tile(i, j), 0)),
                     pl.BlockSpec((tm, LANE), lambda i, j: (pos_tile(i, j), 0))]
        args += [cos, sin]
    osz = jnp.dtype(out_dtype).itemsize
    vmem = 2 * (tm * k * 4 + k * tn * 2 + tm * tn * osz) + tm * k * 2 + 2 * tm * tn * 4 + 4 * MIB
    return pl.pallas_call(
        functools.partial(_norm_matmul_kernel, modulate=mod is not None, rope_half=rope_half),
        grid=(t // tm, n // tn),
        in_specs=in_specs,
        out_specs=pl.BlockSpec((tm, tn), lambda i, j: (i, j)),
        out_shape=jax.ShapeDtypeStruct((t, n), out_dtype),
        scratch_shapes=[pltpu.VMEM((tm, k), BF16)],
        compiler_params=_params(("parallel", "arbitrary"), vmem),
        name=name,
    )(*args)


def _matmul_kernel(*refs, scale, aliased):
    a_ref, b_ref = refs[0], refs[1]
    o_ref, acc_ref = refs[-2], refs[-1]
    del aliased
    kk = pl.program_id(3)

    @pl.when(kk == 0)
    def _():
        acc_ref[...] = jnp.zeros_like(acc_ref)

    acc_ref[...] += jnp.dot(a_ref[...], b_ref[...], preferred_element_type=F32)

    @pl.when(kk == pl.num_programs(3) - 1)
    def _():
        acc = acc_ref[...]
        if scale != 1.0:
            acc = acc * scale
        o_ref[...] = acc.astype(o_ref.dtype)


def _matmul(a, b, *, grid, a_spec, b_spec, o_spec, out_shape, tile, scale=1.0, prev=None, name):
    tm, tn, tk = tile
    in_specs = [a_spec, b_spec]
    args = [a, b]
    aliases = {}
    if prev is not None:
        in_specs.append(pl.BlockSpec(memory_space=pl.ANY))
        args.append(prev)
        aliases = {2: 0}
    osz = jnp.dtype(out_shape.dtype).itemsize
    vmem = 2 * (tm * tk * 2 + tk * tn * 2 + tm * tn * osz) + 2 * tm * tn * 4 + 4 * MIB
    return pl.pallas_call(
        functools.partial(_matmul_kernel, scale=scale, aliased=prev is not None),
        grid=grid,
        in_specs=in_specs,
        out_specs=o_spec,
        out_shape=out_shape,
        scratch_shapes=[pltpu.VMEM((tm, tn), F32)],
        input_output_aliases=aliases,
        compiler_params=_params(("parallel", "parallel", "parallel", "arbitrary"), vmem),
        name=name,
    )(*args)


def _out_proj_kernel(a1_ref, a2_ref, w1_ref, w2_ref, x_ref, g_ref, o_ref):
    acc = jnp.dot(a1_ref[...], w1_ref[...], preferred_element_type=F32)
    acc += jnp.dot(a2_ref[...], w2_ref[...], preferred_element_type=F32)
    o_ref[...] = x_ref[...] + g_ref[...] * acc


def _out_proj(a1, a2, w1, w2, x, mod, layer, m_gate, rows_p, seq_s, *, name):
    t, d = x.shape
    k1, k2 = a1.shape[1], a2.shape[1]
    tm = _tile(math.gcd(rows_p, seq_s), ROW_TILE)
    tn = _tile(d, 2 * COL_TILE, LANE)

    def grp(i):
        return _group_of_row(i * tm, rows_p, seq_s)

    vmem = 2 * (tm * (k1 + k2) * 2 + (k1 + k2) * tn * 2 + 2 * tm * tn * 4) + 2 * tm * tn * 4 + 4 * MIB
    return pl.pallas_call(
        _out_proj_kernel,
        grid=(t // tm, d // tn),
        in_specs=[
            pl.BlockSpec((tm, k1), lambda i, j: (i, 0)),
            pl.BlockSpec((tm, k2), lambda i, j: (i, 0)),
            pl.BlockSpec((k1, tn), lambda i, j: (0, j)),
            pl.BlockSpec((k2, tn), lambda i, j: (0, j)),
            pl.BlockSpec((tm, tn), lambda i, j: (i, j)),
            pl.BlockSpec((None, None, None, 1, tn), lambda i, j: (layer, m_gate, grp(i), 0, j)),
        ],
        out_specs=pl.BlockSpec((tm, tn), lambda i, j: (i, j)),
        out_shape=jax.ShapeDtypeStruct((t, d), F32),
        compiler_params=_params(("parallel", "arbitrary"), vmem),
        name=name,
    )(a1, a2, w1, w2, x, mod)


def _qk_post_kernel(q_ref, k_ref, v_ref, qw_ref, kw_ref, cos_ref, sin_ref,
                    qo_ref, ko_ref, vo_ref, ks_ref):
    cos, sin = cos_ref[...], sin_ref[...]
    qw, kw = qw_ref[...], kw_ref[...]
    for h in range(q_ref.shape[1] // HEAD_DIM):
        sl = slice(h * HEAD_DIM, (h + 1) * HEAD_DIM)
        y = _rope_lanes(_rmsnorm(q_ref[:, sl], qw), cos, sin, HEAD_DIM // 4)
        qo_ref[:, sl] = y.astype(BF16)
    for h in range(k_ref.shape[1] // HEAD_DIM):
        sl = slice(h * HEAD_DIM, (h + 1) * HEAD_DIM)
        y = _rope_lanes(_rmsnorm(k_ref[:, sl], kw), cos, sin, HEAD_DIM // 4)
        ks_ref[:, sl] = y
        ko_ref[:, sl] = y.astype(BF16)
    vo_ref[...] = v_ref[...].astype(BF16)


def _qk_post(proj, q_norm, k_norm, cos, sin, rows_p, seq_s):
    t = proj.shape[0]
    qw_, kvw = A_HEADS * HEAD_DIM, A_KV_HEADS * HEAD_DIM
    tm = _tile(math.gcd(rows_p, seq_s), POST_ROW_TILE)
    p_tiles, s_tiles = rows_p // tm, seq_s // tm
    kblk = qw_ // kvw

    def pos_tile(i):
        return jnp.where(i >= p_tiles, jnp.maximum(i - p_tiles, 0) % s_tiles, s_tiles)

    vmem = 2 * (tm * (qw_ + 2 * kvw) * 4 + tm * (qw_ + 2 * kvw) * 2 + tm * kvw * 4) + 4 * tm * qw_ * 4 + 4 * MIB
    return pl.pallas_call(
        _qk_post_kernel,
        grid=(t // tm,),
        in_specs=[
            pl.BlockSpec((tm, qw_), lambda i: (i, 0)),
            pl.BlockSpec((tm, kvw), lambda i: (i, kblk)),
            pl.BlockSpec((tm, kvw), lambda i: (i, kblk + 1)),
            pl.BlockSpec((1, HEAD_DIM), lambda i: (0, 0)),
            pl.BlockSpec((1, HEAD_DIM), lambda i: (0, 0)),
            pl.BlockSpec((tm, LANE), lambda i: (pos_tile(i), 0)),
            pl.BlockSpec((tm, LANE), lambda i: (pos_tile(i), 0)),
        ],
        out_specs=[
            pl.BlockSpec((tm, qw_), lambda i: (i, 0)),
            pl.BlockSpec((tm, kvw), lambda i: (i, 0)),
            pl.BlockSpec((tm, kvw), lambda i: (i, 0)),
            pl.BlockSpec((tm, kvw), lambda i: (i, 0)),
        ],
        out_shape=[
            jax.ShapeDtypeStruct((t, qw_), BF16),
            jax.ShapeDtypeStruct((t, kvw), BF16),
            jax.ShapeDtypeStruct((t, kvw), BF16),
            jax.ShapeDtypeStruct((t, kvw), F32),
        ],
        compiler_params=_params(("parallel",), vmem),
        name="even_qk_norm_rope",
    )(proj, proj, proj, q_norm.reshape(1, HEAD_DIM), k_norm.reshape(1, HEAD_DIM), cos, sin)


def _attention_kernel(*refs, group, scale, extra, aliased):
    it = iter(refs)
    q_ref, k_ref, v_ref = next(it), next(it), next(it)
    q2_ref = k2_ref = None
    if extra:
        q2_ref, k2_ref = next(it), next(it)
    if aliased:
        next(it)
    o_ref = next(it)
    k2 = k2_ref[...] if extra else None
    for j in range(k_ref.shape[1] // LANE):
        kv_sl = slice(j * LANE, (j + 1) * LANE)
        k = k_ref[:, kv_sl]
        v = v_ref[:, kv_sl]
        if extra:
            k = jnp.concatenate([k, k2], axis=-1)
        for h in range(group):
            sl = slice((j * group + h) * LANE, (j * group + h + 1) * LANE)
            q = q_ref[:, sl]
            if extra:
                q = jnp.concatenate([q, q2_ref[:, sl]], axis=-1)
            s = lax.dot_general(q, k, (((1,), (1,)), ((), ())), preferred_element_type=F32) * (scale * LOG2_E)
            m = jnp.max(s, axis=-1, keepdims=True)
            p = jnp.exp2(s - m)
            den = jnp.sum(p, axis=-1, keepdims=True)
            o = jnp.dot(p.astype(BF16), v, preferred_element_type=F32) / den
            o_ref[:, sl] = o.astype(o_ref.dtype)


def _attention(q, k, v, *, n_batch, sq, nk, q_row0, k_blk0, k_col0, v_col0, n_kv, group, scale,
               out_cols, kv_per_step=1, q2=None, q2_col0=0, k2=None, k2_blk0=0, prev=None, name):
    t = q.shape[0]
    tq = _tile(math.gcd(sq, q_row0), ATTN_Q_TILE)
    qb0 = q_row0 // tq
    qpb = sq // tq
    kps = kv_per_step
    assert n_kv % kps == 0 and k_col0 % kps == 0 and v_col0 % kps == 0 and q2_col0 % (kps * group) == 0
    kw = kps * LANE
    gw = kps * group * LANE
    kc0, vc0, q2c0 = k_col0 // kps, v_col0 // kps, q2_col0 // (kps * group)
    extra = q2 is not None
    in_specs = [
        pl.BlockSpec((tq, gw), lambda b, g, i: (qb0 + b * qpb + i, g)),
        pl.BlockSpec((nk, kw), lambda b, g, i: (k_blk0 + b, kc0 + g)),
        pl.BlockSpec((nk, kw), lambda b, g, i: (k_blk0 + b, vc0 + g)),
    ]
    args = [q, k, v]
    if extra:
        in_specs += [pl.BlockSpec((tq, gw), lambda b, g, i: (qb0 + b * qpb + i, q2c0 + g)),
                     pl.BlockSpec((nk, LANE), lambda b, g, i: (k2_blk0 + b, 0))]
        args += [q, k2]
    aliases = {}
    if prev is not None:
        in_specs.append(pl.BlockSpec(memory_space=pl.ANY))
        aliases = {len(args): 0}
        args.append(prev)
    kd = 2 * LANE if extra else LANE
    vmem = (2 * (tq * gw * 2 * (2 if extra else 1) + nk * kw * 2 * 2 + nk * LANE * 2 + tq * gw * 2)
            + nk * kd * 2 + 3 * kps * tq * nk * 4 + 4 * MIB)
    return pl.pallas_call(
        functools.partial(_attention_kernel, group=group, scale=scale, extra=extra, aliased=prev is not None),
        grid=(n_batch, n_kv // kps, qpb),
        in_specs=in_specs,
        out_specs=pl.BlockSpec((tq, gw), lambda b, g, i: (qb0 + b * qpb + i, g)),
        out_shape=jax.ShapeDtypeStruct((t, out_cols), BF16),
        input_output_aliases=aliases,
        compiler_params=_params(("parallel", "parallel", "parallel"), vmem),
        name=name,
    )(*args)


def _dft_width_kernel(f_ref, cs_ref, ab_ref):
    r = jnp.dot(f_ref[...].astype(BF16), cs_ref[...], preferred_element_type=F32)
    ab_ref[0] = r[:, :B_GROUP_W].astype(BF16)
    ab_ref[1] = r[:, B_GROUP_W:].astype(BF16)


def _dft_matrix(n):
    base = _tile(n, max(SUBLANE, int(math.sqrt(n))))
    k = jnp.arange(n, dtype=I32)

    def tables(rows):
        ang = ((rows[:, None] * k[None, :]) % n).astype(F32) * (2.0 * math.pi / n)
        return jnp.cos(ang), jnp.sin(ang)

    c1, s1 = tables(jnp.arange(n // base, dtype=I32) * base)
    c2, s2 = tables(jnp.arange(base, dtype=I32))
    q1 = jnp.concatenate([c1, s1], axis=1)[:, None, :]
    q2 = jnp.concatenate([-s1, c1], axis=1)[:, None, :]
    r1 = jnp.concatenate([c2, c2], axis=1)[None, :, :]
    r2 = jnp.concatenate([s2, s2], axis=1)[None, :, :]
    return (q1 * r1 + q2 * r2).reshape(n, 2 * n)


def _dft_tables(n):
    j = jnp.arange(n, dtype=I32)
    jk = (j[:, None] * j[None, :]) % n
    ang = jk.astype(F32) * (2.0 * math.pi / n)
    return jnp.cos(ang), jnp.sin(ang)


def _dft_width(proj, col0_blk, rows_p, seq_s):
    t = proj.shape[0]
    bw = B_GROUPS * B_GROUP_W
    cw, sw = _dft_tables(B_GROUP_W)
    cs = jnp.concatenate([cw, -sw], axis=1).astype(BF16)
    tm = _tile(math.gcd(rows_p, seq_s), 2 * ROW_TILE)
    vmem = 2 * (tm * LANE * 4 + 2 * tm * LANE * 2) + 4 * tm * LANE * 4 + 4 * MIB
    return pl.pallas_call(
        _dft_width_kernel,
        grid=(t // tm, B_GROUPS),
        in_specs=[pl.BlockSpec((tm, B_GROUP_W), lambda i, g: (i, col0_blk + g)),
                  pl.BlockSpec((B_GROUP_W, 2 * B_GROUP_W), lambda i, g: (0, 0))],
        out_specs=pl.BlockSpec((2, tm, B_GROUP_W), lambda i, g: (0, i, g)),
        out_shape=jax.ShapeDtypeStruct((2, t, bw), BF16),
        compiler_params=_params(("parallel", "parallel"), vmem),
        name="fourier_width_dft",
    )(proj, cs)


def _dft_seq(ab, n_batch, seq, row0, prev, name):
    _, t, bw = ab.shape
    f = _dft_matrix(seq).astype(BF16)
    tm = _tile(math.gcd(seq, row0), 2 * ROW_TILE)
    tk = tm
    tn = _tile(bw, 2 * COL_TILE, LANE)
    kt = seq // tk
    rb0 = row0 // tk
    ob0 = row0 // tm
    scale = 1.0 / math.sqrt(seq * B_GROUP_W)
    return _matmul(
        f, ab,
        grid=(n_batch, seq // tm, bw // tn, 2 * kt),
        a_spec=pl.BlockSpec((tm, tk), lambda b, i, j, kk: (i, kk)),
        b_spec=pl.BlockSpec((None, tk, tn), lambda b, i, j, kk: (kk // kt, rb0 + b * kt + kk % kt, j)),
        o_spec=pl.BlockSpec((tm, tn), lambda b, i, j, kk: (ob0 + b * (seq // tm) + i, j)),
        out_shape=jax.ShapeDtypeStruct((t, bw), BF16),
        tile=(tm, tn, tk), scale=scale, prev=prev, name=name)


def _kv_post_kernel(ckv_ref, kr_ref, w_ref, cos_ref, sin_ref, cn_ref, cb_ref, kr_out_ref):
    y = _rmsnorm(ckv_ref[...], w_ref[...])
    cn_ref[...] = y
    cb_ref[...] = y.astype(BF16)
    kr_out_ref[...] = _rope_lanes(kr_ref[...], cos_ref[...], sin_ref[...], QK_ROPE // 4).astype(BF16)


def _kv_post(proj, ckv_blk, kr_blk, kv_norm, cos, sin, rows_p, seq_s):
    t = proj.shape[0]
    tm = _tile(math.gcd(rows_p, seq_s), POST_ROW_TILE)
    p_tiles, s_tiles = rows_p // tm, seq_s // tm

    def pos_tile(i):
        return jnp.where(i >= p_tiles, jnp.maximum(i - p_tiles, 0) % s_tiles, s_tiles)

    vmem = 2 * (tm * KV_LORA * 10 + tm * LANE * 14) + 4 * tm * KV_LORA * 4 + 4 * MIB
    return pl.pallas_call(
        _kv_post_kernel,
        grid=(t // tm,),
        in_specs=[
            pl.BlockSpec((tm, KV_LORA), lambda i: (i, ckv_blk)),
            pl.BlockSpec((tm, LANE), lambda i: (i, kr_blk)),
            pl.BlockSpec((1, KV_LORA), lambda i: (0, 0)),
            pl.BlockSpec((tm, LANE), lambda i: (pos_tile(i), 0)),
            pl.BlockSpec((tm, LANE), lambda i: (pos_tile(i), 0)),
        ],
        out_specs=[
            pl.BlockSpec((tm, KV_LORA), lambda i: (i, 0)),
            pl.BlockSpec((tm, KV_LORA), lambda i: (i, 0)),
            pl.BlockSpec((tm, LANE), lambda i: (i, 0)),
        ],
        out_shape=[
            jax.ShapeDtypeStruct((t, KV_LORA), F32),
            jax.ShapeDtypeStruct((t, KV_LORA), BF16),
            jax.ShapeDtypeStruct((t, LANE), BF16),
        ],
        compiler_params=_params(("parallel",), vmem),
        name="odd_kv_norm_rope",
    )(proj, proj, kv_norm.reshape(1, KV_LORA), cos, sin)


def _conv_kernel(u_ref, gb_ref, gc_ref, up_ref, gcp_ref, un_ref, gcn_ref, w_ref, o_ref, *,
                 tr, rows_p, seq_p, seq_s):
    i = pl.program_id(0)
    start = i * tr
    local = jnp.where(start < rows_p, start % seq_p, jnp.maximum(start - rows_p, 0) % seq_s)
    seq = jnp.where(start < rows_p, seq_p, seq_s)
    has_prev = local > 0
    has_next = local + tr < seq
    z = gc_ref[...] * u_ref[...]
    zp = (gcp_ref[...] * up_ref[...])[SUBLANE - 1:SUBLANE, :]
    zn = (gcn_ref[...] * un_ref[...])[0:1, :]
    zp = jnp.where(has_prev, zp, 0.0)
    zn = jnp.where(has_next, zn, 0.0)
    row = lax.broadcasted_iota(I32, z.shape, 0)
    z_prev = jnp.where(row == 0, zp, pltpu.roll(z, 1, 0))
    z_next = jnp.where(row == tr - 1, zn, pltpu.roll(z, tr - 1, 0))
    w = w_ref[...]
    conv = z_prev * w[0:1, :] + z * w[1:2, :] + z_next * w[2:3, :]
    o_ref[...] = (gb_ref[...] * conv).astype(o_ref.dtype)


def _gated_conv(proj, u_blk, gb_blk, gc_blk, conv_w, rows_p, seq_p, seq_s):
    t = proj.shape[0]
    tr = _tile(math.gcd(seq_p, seq_s), POST_ROW_TILE)
    tc = _tile(D_CH, COL_TILE, LANE)
    halo = tr // SUBLANE
    last = t // SUBLANE - 1
    cpb = D_CH // tc

    def main(blk):
        return pl.BlockSpec((tr, tc), lambda i, j: (i, blk * cpb + j))

    def prev(blk):
        return pl.BlockSpec((SUBLANE, tc), lambda i, j: (jnp.maximum(i * halo - 1, 0), blk * cpb + j))

    def nxt(blk):
        return pl.BlockSpec((SUBLANE, tc), lambda i, j: (jnp.minimum((i + 1) * halo, last), blk * cpb + j))

    w8 = jnp.pad(conv_w, ((0, SUBLANE - CONV_W), (0, 0)))
    vmem = 2 * (3 * tr * tc * 4 + 4 * SUBLANE * tc * 4 + tr * tc * 2) + 8 * tr * tc * 4 + 4 * MIB
    return pl.pallas_call(
        functools.partial(_conv_kernel, tr=tr, rows_p=rows_p, seq_p=seq_p, seq_s=seq_s),
        grid=(t // tr, cpb),
        in_specs=[main(u_blk), main(gb_blk), main(gc_blk), prev(u_blk), prev(gc_blk), nxt(u_blk), nxt(gc_blk),
                  pl.BlockSpec((SUBLANE, tc), lambda i, j: (0, j))],
        out_specs=pl.BlockSpec((tr, tc), lambda i, j: (i, j)),
        out_shape=jax.ShapeDtypeStruct((t, D_CH), BF16),
        compiler_params=_params(("parallel", "parallel"), vmem),
        name="odd_gated_conv",
    )(proj, proj, proj, proj, proj, proj, proj, w8)


def _router_kernel(x_ref, nw_ref, sc_ref, sh_ref, rw_ref, rb_ref,
                   h_ref, idx_ref, wt_ref, rank_ref, cnt_ref, carry_ref):
    @pl.when(pl.program_id(0) == 0)
    def _():
        carry_ref[...] = jnp.zeros_like(carry_ref)

    h = _rmsnorm(x_ref[...], nw_ref[...]) * (1.0 + sc_ref[...]) + sh_ref[...]
    for q in range(h.shape[1] // (2 * LANE)):
        k, lanes = _pair_position(q, h_ref.shape[1])
        lo = h[:, (2 * q) * LANE:(2 * q + 1) * LANE]
        hi = h[:, (2 * q + 1) * LANE:(2 * q + 2) * LANE]
        h_ref[pl.ds(k, h.shape[0], stride=SLAB_ROWS), lanes] = _pack_pair(lo, hi)
    logits = lax.dot_general(rw_ref[...], h, (((1,), (1,)), ((), ())),
                             precision=lax.Precision.HIGHEST, preferred_element_type=F32) + rb_ref[...]
    n_e, tm = logits.shape
    eidx = lax.broadcasted_iota(I32, logits.shape, 0).astype(F32)
    work = logits
    vals, hots = [], []
    for k in range(TOP_K):
        m = jnp.max(work, axis=0, keepdims=True)
        sel = jnp.min(jnp.where(work == m, eidx, float(n_e)), axis=0, keepdims=True)
        hot = eidx == sel
        idx_ref[k:k + 1, :] = sel.astype(I32)
        vals.append(m)
        hots.append(hot)
        work = jnp.where(hot, -jnp.inf, work)
    exps = [jnp.exp(v - vals[0]) for v in vals]
    den = exps[0]
    for e in exps[1:]:
        den = den + e
    for k in range(TOP_K):
        wt_ref[k:k + 1, :] = exps[k] / den
    chosen = hots[0]
    for hot in hots[1:]:
        chosen = jnp.logical_or(chosen, hot)
    chosen = jnp.where(chosen, 1.0, 0.0)
    r = lax.broadcasted_iota(I32, (tm, tm), 0)
    c = lax.broadcasted_iota(I32, (tm, tm), 1)
    upper = jnp.where(r < c, 1.0, 0.0).astype(BF16)
    before = jnp.dot(chosen.astype(BF16), upper, preferred_element_type=F32) + carry_ref[:, 0:1]
    for k in range(TOP_K):
        rk = jnp.sum(jnp.where(hots[k], before, 0.0), axis=0, keepdims=True)
        rank_ref[k:k + 1, :] = rk.astype(I32)
    carry_ref[...] = carry_ref[...] + jnp.sum(chosen, axis=1, keepdims=True)
    cnt_ref[...] = carry_ref[...].astype(I32)


def _router(x, nw, mod, layer, router_w, router_b, rows_p, seq_s):
    t, d = x.shape
    n_e = router_w.shape[1]
    tm = _tile(math.gcd(rows_p, seq_s), ROW_TILE, LANE)
    width = _packed_width(d)

    def grp(i):
        return _group_of_row(i * tm, rows_p, seq_s)

    vmem = 2 * (tm * d * 6 + n_e * d * 4) + 4 * tm * d * 4 + tm * tm * 4 + 4 * MIB
    return pl.pallas_call(
        _router_kernel,
        grid=(t // tm,),
        in_specs=[
            pl.BlockSpec((tm, d), lambda i: (i, 0)),
            pl.BlockSpec((1, d), lambda i: (0, 0)),
            pl.BlockSpec((None, None, None, 1, d), lambda i: (layer, 4, grp(i), 0, 0)),
            pl.BlockSpec((None, None, None, 1, d), lambda i: (layer, 3, grp(i), 0, 0)),
            pl.BlockSpec((n_e, d), lambda i: (0, 0)),
            pl.BlockSpec((n_e, 1), lambda i: (0, 0)),
        ],
        out_specs=[
            pl.BlockSpec((tm * SLAB_ROWS, width), lambda i: (i, 0)),
            pl.BlockSpec((TOP_K, tm), lambda i: (0, i)),
            pl.BlockSpec((TOP_K, tm), lambda i: (0, i)),
            pl.BlockSpec((TOP_K, tm), lambda i: (0, i)),
            pl.BlockSpec((n_e, LANE), lambda i: (0, 0)),
        ],
        out_shape=[
            jax.ShapeDtypeStruct((t * SLAB_ROWS, width), jnp.uint32),
            jax.ShapeDtypeStruct((TOP_K, t), I32),
            jax.ShapeDtypeStruct((TOP_K, t), F32),
            jax.ShapeDtypeStruct((TOP_K, t), I32),
            jax.ShapeDtypeStruct((n_e, LANE), I32),
        ],
        scratch_shapes=[pltpu.VMEM((n_e, LANE), F32)],
        compiler_params=_params(("arbitrary",), vmem),
        name="moe_router",
    )(x, nw.reshape(1, d), mod, mod, router_w.T, router_b.reshape(n_e, 1))


def _slab_copy(src_hbm, row, dst, r, sem):
    return pltpu.make_async_copy(src_hbm.at[pl.ds(pl.multiple_of(row * SLAB_ROWS, SLAB_ROWS), SLAB_ROWS), :],
                                 dst.at[pl.ds(pl.multiple_of(r * SLAB_ROWS, SLAB_ROWS), SLAB_ROWS), :], sem)


def _gather_kernel(nv_ref, pairs_ref, tok_ref, tok_next_ref, h_hbm, o_ref, buf0, buf1, sem):
    i = pl.program_id(0)
    nv = nv_ref[0]
    rows = o_ref.shape[0]
    bufs = (buf0, buf1)
    width = buf0.shape[1]

    def start_tile(tok, slot, n_pairs):
        def body(r2, carry):
            for u in range(2):
                r = 2 * r2 + u
                _slab_copy(h_hbm, tok[0, r], bufs[slot], r, sem.at[slot]).start(priority=u)
            return carry
        lax.fori_loop(0, n_pairs, body, 0)

    @pl.when(i == 0)
    def _():
        buf0[...] = jnp.zeros_like(buf0)
        buf1[...] = jnp.zeros_like(buf1)
        start_tile(tok_ref, 0, pairs_ref[0])

    for par in (0, 1):
        @pl.when(jnp.logical_and(i % 2 == par, i + 1 < nv))
        def _():
            start_tile(tok_next_ref, 1 - par, pairs_ref[jnp.minimum(i + 1, nv - 1)])

        @pl.when(jnp.logical_and(i % 2 == par, i < nv))
        def _():
            def body(r, carry):
                _slab_copy(h_hbm, 0, bufs[par], r, sem.at[par]).wait()
                return carry
            lax.fori_loop(0, 2 * pairs_ref[i], body, 0)
            for q in range(o_ref.shape[1] // (2 * LANE)):
                k, lanes = _pair_position(q, width)
                lo, hi = _unpack_pair(bufs[par][pl.ds(k, rows, stride=SLAB_ROWS), lanes])
                o_ref[:, (2 * q) * LANE:(2 * q + 1) * LANE] = lo.astype(o_ref.dtype)
                o_ref[:, (2 * q + 1) * LANE:(2 * q + 2) * LANE] = hi.astype(o_ref.dtype)

    @pl.when(i >= nv)
    def _():
        o_ref[...] = jnp.zeros_like(o_ref)


def _gather_rows(h_packed, d, token_of, n_valid, tile_pairs, tg):
    nt = token_of.shape[0]
    width = _packed_width(d)
    assert tg % 2 == 0
    vmem = 2 * tg * d * 2 + 2 * tg * d * 2 + 2 * tg * d * 4 + 4 * MIB
    grid_spec = pltpu.PrefetchScalarGridSpec(
        num_scalar_prefetch=2,
        grid=(nt,),
        in_specs=[
            pl.BlockSpec((None, 1, tg), lambda i, nv, tp: (jnp.minimum(i, nv[0] - 1), 0, 0),
                         memory_space=pltpu.SMEM),
            pl.BlockSpec((None, 1, tg), lambda i, nv, tp: (jnp.minimum(i + 1, nv[0] - 1), 0, 0),
                         memory_space=pltpu.SMEM),
            pl.BlockSpec(memory_space=pl.ANY),
        ],
        out_specs=pl.BlockSpec((tg, d), lambda i, nv, tp: (i, 0)),
        scratch_shapes=[pltpu.VMEM((tg * SLAB_ROWS, width), jnp.uint32),
                        pltpu.VMEM((tg * SLAB_ROWS, width), jnp.uint32), pltpu.SemaphoreType.DMA((2,))],
    )
    return pl.pallas_call(
        _gather_kernel,
        grid_spec=grid_spec,
        out_shape=jax.ShapeDtypeStruct((nt * tg, d), BF16),
        compiler_params=_params(("arbitrary",), vmem),
        name="moe_dispatch_gather",
    )(n_valid, tile_pairs, token_of, token_of, h_packed)


def _gate_up_kernel(te_ref, nv_ref, x_ref, wg_ref, wu_ref, bg_ref, bu_ref, o_ref, wg_bf, wu_bf):
    r = pl.program_id(1)
    e = te_ref[r]
    e_prev = te_ref[jnp.maximum(r - 1, 0)]

    @pl.when(jnp.logical_or(r == 0, e != e_prev))
    def _():
        wg_bf[...] = wg_ref[...].astype(BF16)
        wu_bf[...] = wu_ref[...].astype(BF16)

    @pl.when(r < nv_ref[0])
    def _():
        x = x_ref[...]
        g = jnp.dot(x, wg_bf[...], preferred_element_type=F32) + bg_ref[...]
        u = jnp.dot(x, wu_bf[...], preferred_element_type=F32) + bu_ref[...]
        g = jnp.minimum(g, SWIGLU_LIMIT)
        u = jnp.clip(u, -SWIGLU_LIMIT, SWIGLU_LIMIT)
        o_ref[...] = ((u + 1.0) * g * jax.nn.sigmoid(SWIGLU_ALPHA * g)).astype(o_ref.dtype)

    @pl.when(r >= nv_ref[0])
    def _():
        o_ref[...] = jnp.zeros_like(o_ref)


def _gate_up(xs, tile_expert, n_valid, layer, w_gate, w_up, b_gate, b_up, tm):
    p, d = xs.shape
    depth, n_e, _, ff = w_gate.shape
    nt = p // tm
    nc = _tile(ff, GATE_UP_COL_TILE, LANE)

    def row(c, r, te, nv):
        return jnp.minimum(r, nv[0] - 1)

    vmem = 2 * (tm * d * 2 + 2 * d * nc * 4 + tm * nc * 2) + 2 * d * nc * 2 + 6 * tm * nc * 4 + 4 * MIB
    grid_spec = pltpu.PrefetchScalarGridSpec(
        num_scalar_prefetch=2,
        grid=(ff // nc, nt),
        in_specs=[
            pl.BlockSpec((tm, d), lambda c, r, te, nv: (row(c, r, te, nv), 0)),
            pl.BlockSpec((None, None, d, nc), lambda c, r, te, nv: (layer, te[r], 0, c)),
            pl.BlockSpec((None, None, d, nc), lambda c, r, te, nv: (layer, te[r], 0, c)),
            pl.BlockSpec((None, None, 1, nc), lambda c, r, te, nv: (layer, te[r], 0, c)),
            pl.BlockSpec((None, None, 1, nc), lambda c, r, te, nv: (layer, te[r], 0, c)),
        ],
        out_specs=pl.BlockSpec((tm, nc), lambda c, r, te, nv: (r, c)),
        scratch_shapes=[pltpu.VMEM((d, nc), BF16), pltpu.VMEM((d, nc), BF16)],
    )
    return pl.pallas_call(
        _gate_up_kernel,
        grid_spec=grid_spec,
        out_shape=jax.ShapeDtypeStruct((p, ff), BF16),
        compiler_params=_params(("arbitrary", "arbitrary"), vmem),
        name="moe_gate_up",
    )(tile_expert, n_valid, xs, w_gate, w_up, b_gate.reshape(depth, n_e, 1, ff), b_up.reshape(depth, n_e, 1, ff))


def _down_kernel(te_ref, nv_ref, a_ref, wd_ref, bd_ref, o_ref, wd_bf):
    r = pl.program_id(0)
    e = te_ref[r]
    e_prev = te_ref[jnp.maximum(r - 1, 0)]
    rows = a_ref.shape[0]

    @pl.when(jnp.logical_or(r == 0, e != e_prev))
    def _():
        wd_bf[...] = wd_ref[...].astype(BF16)

    @pl.when(r < nv_ref[0])
    def _():
        a = a_ref[...]
        for q in range(wd_bf.shape[1] // (2 * LANE)):
            sl = slice(2 * q * LANE, (2 * q + 2) * LANE)
            y = jnp.dot(a, wd_bf[:, sl], preferred_element_type=F32) + bd_ref[:, sl]
            k, lanes = _pair_position(q, o_ref.shape[1])
            o_ref[pl.ds(k, rows, stride=SLAB_ROWS), lanes] = _pack_pair(y[:, :LANE], y[:, LANE:])

    @pl.when(r >= nv_ref[0])
    def _():
        o_ref[...] = jnp.zeros_like(o_ref)


def _down(act, tile_expert, n_valid, layer, w_down, b_down, tm):
    p, ff = act.shape
    depth, n_e, _, d = w_down.shape
    nt = p // tm
    width = _packed_width(d)
    vmem = 2 * (tm * ff * 2 + tm * d * 2 + ff * d * 4) + ff * d * 2 + 4 * tm * 2 * LANE * 4 + 4 * MIB
    grid_spec = pltpu.PrefetchScalarGridSpec(
        num_scalar_prefetch=2,
        grid=(nt,),
        in_specs=[
            pl.BlockSpec((tm, ff), lambda r, te, nv: (jnp.minimum(r, nv[0] - 1), 0)),
            pl.BlockSpec((None, None, ff, d), lambda r, te, nv: (layer, te[r], 0, 0)),
            pl.BlockSpec((None, None, 1, d), lambda r, te, nv: (layer, te[r], 0, 0)),
        ],
        out_specs=pl.BlockSpec((tm * SLAB_ROWS, width), lambda r, te, nv: (r, 0)),
        scratch_shapes=[pltpu.VMEM((ff, d), BF16)],
    )
    return pl.pallas_call(
        _down_kernel,
        grid_spec=grid_spec,
        out_shape=jax.ShapeDtypeStruct((p * SLAB_ROWS, width), jnp.uint32),
        compiler_params=_params(("arbitrary",), vmem),
        name="moe_down",
    )(tile_expert, n_valid, act, w_down, b_down.reshape(depth, n_e, 1, d))


def _combine_kernel(pos_ref, pos_next_ref, x_ref, wt_ref, g_ref, fw_ref, y_hbm, *rest, p_tiles):
    if p_tiles is None:
        o_ref, buf0, buf1, sem = rest
    else:
        oc_ref, ol_ref, buf0, buf1, sem, o_ref = rest
    i = pl.program_id(0)
    n = pl.num_programs(0)
    rows = x_ref.shape[0]
    bufs = (buf0, buf1)
    width = buf0.shape[1]

    def start_tile(pos, slot):
        def body(r, carry):
            for k in range(TOP_K):
                _slab_copy(y_hbm, pos[k, r], bufs[slot], k * rows + r, sem.at[slot]).start(priority=k % 2)
            return carry
        lax.fori_loop(0, rows, body, 0, unroll=4)

    @pl.when(i == 0)
    def _():
        start_tile(pos_ref, 0)

    for par in (0, 1):
        @pl.when(jnp.logical_and(i % 2 == par, i + 1 < n))
        def _():
            start_tile(pos_next_ref, 1 - par)

        @pl.when(i % 2 == par)
        def _():
            def wait_body(r, carry):
                for k in range(TOP_K):
                    _slab_copy(y_hbm, 0, bufs[par], k * rows + r, sem.at[par]).wait()
                return carry
            lax.fori_loop(0, rows, wait_body, 0, unroll=4)
            wt = wt_ref[...]
            wb = [jnp.broadcast_to(wt[:, k:k + 1], (rows, LANE)) for k in range(TOP_K)]
            for c in range(x_ref.shape[1] // (2 * LANE)):
                row, lanes = _pair_position(c, width)
                acc_lo = acc_hi = None
                for k in range(TOP_K):
                    lo, hi = _unpack_pair(
                        bufs[par][pl.ds(k * rows * SLAB_ROWS + row, rows, stride=SLAB_ROWS), lanes])
                    acc_lo = wb[k] * lo if acc_lo is None else acc_lo + wb[k] * lo
                    acc_hi = wb[k] * hi if acc_hi is None else acc_hi + wb[k] * hi
                s_lo = slice((2 * c) * LANE, (2 * c + 1) * LANE)
                s_hi = slice((2 * c + 1) * LANE, (2 * c + 2) * LANE)
                o_ref[:, s_lo] = x_ref[:, s_lo] + g_ref[:, s_lo] * acc_lo
                o_ref[:, s_hi] = x_ref[:, s_hi] + g_ref[:, s_hi] * acc_hi

    if p_tiles is not None:
        @pl.when(i < p_tiles)
        def _():
            oc_ref[...] = _rmsnorm(o_ref[...], fw_ref[...])

        @pl.when(i >= p_tiles)
        def _():
            ol_ref[...] = _rmsnorm(o_ref[...], fw_ref[...])


def _combine(x, y_pairs, pos, wts, mod, layer, final_w, rows_p, seq_s, *, final_norm):
    t, d = x.shape
    tc = _tile(math.gcd(rows_p, seq_s), COMBINE_ROW_TILE)
    nt = t // tc
    pos3 = pos.reshape(TOP_K, nt, tc).transpose(1, 0, 2)

    def grp(i):
        return _group_of_row(i * tc, rows_p, seq_s)

    width = _packed_width(d)
    vmem = 2 * TOP_K * tc * d * 2 + 2 * (3 * tc * d * 4 + tc * LANE * 4) + 4 * tc * d * 4 + 4 * MIB
    scratch = [pltpu.VMEM((TOP_K * tc * SLAB_ROWS, width), jnp.uint32),
               pltpu.VMEM((TOP_K * tc * SLAB_ROWS, width), jnp.uint32), pltpu.SemaphoreType.DMA((2,))]
    if final_norm:
        p_tiles = rows_p // tc
        out_specs = [pl.BlockSpec((tc, d), lambda i: (jnp.minimum(i, p_tiles - 1), 0)),
                     pl.BlockSpec((tc, d), lambda i: (jnp.maximum(i - p_tiles, 0), 0))]
        out_shape = [jax.ShapeDtypeStruct((rows_p, d), F32), jax.ShapeDtypeStruct((t - rows_p, d), F32)]
        scratch.append(pltpu.VMEM((tc, d), F32))
    else:
        p_tiles = None
        out_specs = pl.BlockSpec((tc, d), lambda i: (i, 0))
        out_shape = jax.ShapeDtypeStruct((t, d), F32)
    return pl.pallas_call(
        functools.partial(_combine_kernel, p_tiles=p_tiles),
        grid=(nt,),
        in_specs=[
            pl.BlockSpec((None, TOP_K, tc), lambda i: (i, 0, 0), memory_space=pltpu.SMEM),
            pl.BlockSpec((None, TOP_K, tc), lambda i: (jnp.minimum(i + 1, nt - 1), 0, 0), memory_space=pltpu.SMEM),
            pl.BlockSpec((tc, d), lambda i: (i, 0)),
            pl.BlockSpec((tc, TOP_K), lambda i: (i, 0)),
            pl.BlockSpec((None, None, None, 1, d), lambda i: (layer, 5, grp(i), 0, 0)),
            pl.BlockSpec((1, d), lambda i: (0, 0)),
            pl.BlockSpec(memory_space=pl.ANY),
        ],
        out_specs=out_specs,
        out_shape=out_shape,
        scratch_shapes=scratch,
        compiler_params=_params(("arbitrary",), vmem),
        name="moe_combine",
    )(pos3, pos3, x, wts.T, mod, final_w.reshape(1, d), y_pairs)


def _moe_layer(x, nw, mod, layer, router_w, router_b, w_gate, b_gate, w_up, b_up, w_down, b_down,
               final_w, rows_p, seq_s, *, final_norm):
    t, d = x.shape
    n_e = router_w.shape[1]
    tm = EXPERT_ROW_TILE
    tg = _tile(tm, GATHER_ROW_TILE)
    h, idx, wts, rank, cnt = _router(x, nw, mod, layer, router_w, router_b, rows_p, seq_s)
    counts = cnt[:, 0]
    tiles_e = (counts + tm - 1) // tm
    e_ids = jnp.arange(n_e, dtype=I32)
    tile_end = jnp.sum(jnp.where(e_ids[None, :] <= e_ids[:, None], tiles_e[None, :], 0), axis=1)
    offs = (tile_end - tiles_e) * tm
    pos = jnp.sum(jnp.where(idx[:, :, None] == e_ids[None, None, :], offs[None, None, :], 0), axis=-1) + rank
    n_tiles = (t * TOP_K + tm - 1) // tm + n_e
    p_rows = n_tiles * tm
    n_valid_tiles = tile_end[-1]
    tok = jnp.broadcast_to(jnp.arange(t, dtype=I32)[None, :], (TOP_K, t))
    token_of = jnp.zeros((p_rows,), I32).at[pos.reshape(-1)].set(tok.reshape(-1))
    tile_ids = jnp.minimum(jnp.arange(n_tiles, dtype=I32), n_valid_tiles - 1)
    tile_expert = jnp.sum((tile_end[None, :] <= tile_ids[:, None]).astype(I32), axis=1)
    nv_m = n_valid_tiles.reshape(1).astype(I32)
    nv_g = (n_valid_tiles * (tm // tg)).reshape(1).astype(I32)
    g_start = jnp.arange(p_rows // tg, dtype=I32) * tg
    g_expert = jnp.repeat(tile_expert, tm // tg)
    hot = g_expert[:, None] == e_ids[None, :]
    g_end = jnp.sum(jnp.where(hot, (offs + counts)[None, :], 0), axis=1)
    g_rows = jnp.clip(g_end - g_start, 0, tg)
    tile_pairs = ((g_rows + 1) // 2).astype(I32)
    xs = _gather_rows(h, d, token_of.reshape(p_rows // tg, 1, tg), nv_g, tile_pairs, tg)
    act = _gate_up(xs, tile_expert, nv_m, layer, w_gate, w_up, b_gate, b_up, tm)
    y_pairs = _down(act, tile_expert, nv_m, layer, w_down, b_down, tm)
    return _combine(x, y_pairs, pos, wts, mod, layer, final_w, rows_p, seq_s, final_norm=final_norm)


def _even_mixer(x, mod, layer, nw, w_in, q_norm, k_norm, w_out, cache_k, cache_v, dims):
    rows_p, seq_p, n_bp, seq_s, n_bs = dims
    t = x.shape[0]
    qw_, kvw = A_HEADS * HEAD_DIM, A_KV_HEADS * HEAD_DIM
    past = cache_k.shape[1]
    nk_s = past + seq_s
    proj = _norm_matmul(x, 0, nw, w_in.astype(BF16), rows_p, seq_s, mod=mod, layer=layer,
                        m_shift=0, m_scale=1, name="even_in_proj")
    tm_post = _tile(math.gcd(rows_p, seq_s), POST_ROW_TILE)
    cos, sin = _rope_tables(seq_s, HEAD_DIM, tm_post)
    qh, kh, vh, kst = _qk_post(proj, q_norm, k_norm, cos, sin, rows_p, seq_s)
    scale = HEAD_DIM ** -0.5
    group = A_HEADS // A_KV_HEADS
    attn = _attention(qh, kh, vh, n_batch=n_bp, sq=seq_p, nk=seq_p, q_row0=0, k_blk0=0, k_col0=0, v_col0=0,
                      n_kv=A_KV_HEADS, group=group, scale=scale, out_cols=qw_, prev=jnp.zeros((t, qw_), BF16),
                      kv_per_step=_tile(A_KV_HEADS, CTX_KV_HEADS_PER_STEP, 1), name="even_attention_ctx")
    k_lat = kh[rows_p:].reshape(n_bs, seq_s, kvw)
    v_lat = vh[rows_p:].reshape(n_bs, seq_s, kvw)
    k_all = jnp.concatenate([cache_k.reshape(n_bs, past, kvw).astype(BF16), k_lat], axis=1).reshape(n_bs * nk_s, kvw)
    v_all = jnp.concatenate([cache_v.reshape(n_bs, past, kvw).astype(BF16), v_lat], axis=1).reshape(n_bs * nk_s, kvw)
    attn = _attention(qh, k_all, v_all, n_batch=n_bs, sq=seq_s, nk=nk_s, q_row0=rows_p, k_blk0=0, k_col0=0,
                      v_col0=0, n_kv=A_KV_HEADS, group=group, scale=scale, out_cols=qw_, prev=attn,
                      kv_per_step=_tile(A_KV_HEADS, LATENT_GQA_KV_HEADS_PER_STEP, 1), name="even_attention_latent")
    ab = _dft_width(proj, (qw_ + 2 * kvw) // B_GROUP_W, rows_p, seq_s)
    four = _dft_seq(ab, n_bp, seq_p, 0, jnp.zeros((t, B_GROUPS * B_GROUP_W), BF16), "fourier_seq_dft_ctx")
    four = _dft_seq(ab, n_bs, seq_s, rows_p, four, "fourier_seq_dft_latent")
    w_out_bf = w_out.astype(BF16)
    x_new = _out_proj(attn, four, w_out_bf[:qw_], w_out_bf[qw_:], x, mod, layer, 2, rows_p, seq_s,
                      name="even_out_proj")
    k_state = kst[:rows_p].reshape(n_bp, seq_p, A_KV_HEADS, HEAD_DIM)
    v_state = proj[:rows_p, qw_ + kvw:qw_ + 2 * kvw].reshape(n_bp, seq_p, A_KV_HEADS, HEAD_DIM)
    return x_new, k_state, v_state


def _odd_mixer(x, mod, layer, nw, w_in, q_a_norm, kv_a_norm, w_uq, w_ukv, conv_w, w_out,
               cache_ckv, cache_kr, dims):
    rows_p, seq_p, n_bp, seq_s, n_bs = dims
    d = x.shape[1]
    past = cache_ckv.shape[1]
    nk_s = past + seq_s
    o1, o2, o3 = Q_LORA, Q_LORA + KV_LORA, Q_LORA + KV_LORA + QK_ROPE
    o4, o5 = o3 + D_CH, o3 + 2 * D_CH
    pad_w = LANE - QK_ROPE
    w_in_r = jnp.concatenate([w_in[:, o3:o4], w_in[:, o4:o5], w_in[:, o5:], w_in[:, o1:o2], w_in[:, :o1],
                              w_in[:, o2:o3], jnp.zeros((d, pad_w), w_in.dtype)], axis=1).astype(BF16)
    c_ckv = 3 * D_CH
    c_cq = c_ckv + KV_LORA
    c_kr = c_cq + Q_LORA
    assert c_ckv % KV_LORA == 0 and c_cq % Q_LORA == 0 and c_kr % LANE == 0
    proj = _norm_matmul(x, 0, nw, w_in_r, rows_p, seq_s, mod=mod, layer=layer, m_shift=0, m_scale=1,
                        name="odd_in_proj")
    hq = QK_NOPE + QK_ROPE
    wq = w_uq.reshape(Q_LORA, C_HEADS, hq)
    wq_rope = jnp.pad(wq[:, :, QK_NOPE:], ((0, 0), (0, 0), (0, pad_w)))
    wq_r = jnp.concatenate([wq[:, :, :QK_NOPE].reshape(Q_LORA, C_HEADS * QK_NOPE),
                            wq_rope.reshape(Q_LORA, C_HEADS * LANE)], axis=1).astype(BF16)
    tm = _tile(math.gcd(rows_p, seq_s), IN_PROJ_ROW_TILE)
    tn = _tile(wq_r.shape[1], IN_PROJ_COL_TILE, LANE)
    assert (C_HEADS * QK_NOPE) % tn == 0
    cos_q, sin_q = _rope_tables(seq_s, QK_ROPE, tm)
    q = _norm_matmul(proj, c_cq // Q_LORA, q_a_norm, wq_r, rows_p, seq_s,
                     rope=(cos_q, sin_q, QK_ROPE // 4, (C_HEADS * QK_NOPE) // tn), out_dtype=BF16,
                     name="odd_q_up_proj")
    tm_post = _tile(math.gcd(rows_p, seq_s), POST_ROW_TILE)
    cos_k, sin_k = _rope_tables(seq_s, QK_ROPE, tm_post)
    ckvn, ckvn_bf, kr = _kv_post(proj, c_ckv // KV_LORA, c_kr // LANE, kv_a_norm, cos_k, sin_k, rows_p, seq_s)
    hkv = QK_NOPE + V_HEAD
    wkv = w_ukv.reshape(KV_LORA, C_HEADS, hkv)
    wkv_r = jnp.concatenate([wkv[:, :, :QK_NOPE].reshape(KV_LORA, C_HEADS * QK_NOPE),
                             wkv[:, :, QK_NOPE:].reshape(KV_LORA, C_HEADS * V_HEAD)], axis=1).astype(BF16)
    n_kv_cols = wkv_r.shape[1]
    ckv_all = jnp.concatenate([cache_ckv.astype(BF16), ckvn_bf[rows_p:].reshape(n_bs, seq_s, KV_LORA)],
                              axis=1).reshape(n_bs * nk_s, KV_LORA)
    kr_cache = jnp.pad(cache_kr, ((0, 0), (0, 0), (0, pad_w))).astype(BF16)
    kr_all = jnp.concatenate([kr_cache, kr[rows_p:].reshape(n_bs, seq_s, LANE)], axis=1).reshape(n_bs * nk_s, LANE)

    def kv_up(lat, name):
        m = lat.shape[0]
        tmm = _tile(m, ROW_TILE)
        tnn = _tile(n_kv_cols, 2 * COL_TILE, LANE)
        return _matmul(lat, wkv_r, grid=(1, m // tmm, n_kv_cols // tnn, 1),
                       a_spec=pl.BlockSpec((tmm, KV_LORA), lambda b, i, j, kk: (i, 0)),
                       b_spec=pl.BlockSpec((KV_LORA, tnn), lambda b, i, j, kk: (0, j)),
                       o_spec=pl.BlockSpec((tmm, tnn), lambda b, i, j, kk: (i, j)),
                       out_shape=jax.ShapeDtypeStruct((m, n_kv_cols), BF16),
                       tile=(tmm, tnn, KV_LORA), name=name)

    kv_p = kv_up(ckvn_bf[:rows_p], "odd_kv_up_proj_ctx")
    kv_s = kv_up(ckv_all, "odd_kv_up_proj_latent")
    scale = hq ** -0.5
    out_cols = C_HEADS * V_HEAD
    attn = _attention(q, kv_p, kv_p, n_batch=n_bp, sq=seq_p, nk=seq_p, q_row0=0, k_blk0=0, k_col0=0,
                      v_col0=C_HEADS, n_kv=C_HEADS, group=1, scale=scale, out_cols=out_cols,
                      q2=q, q2_col0=C_HEADS, k2=kr, k2_blk0=0, prev=jnp.zeros((x.shape[0], out_cols), BF16),
                      kv_per_step=_tile(C_HEADS, CTX_KV_HEADS_PER_STEP, 1), name="odd_attention_ctx")
    attn = _attention(q, kv_s, kv_s, n_batch=n_bs, sq=seq_s, nk=nk_s, q_row0=rows_p, k_blk0=0, k_col0=0,
                      v_col0=C_HEADS, n_kv=C_HEADS, group=1, scale=scale, out_cols=out_cols,
                      q2=q, q2_col0=C_HEADS, k2=kr_all, k2_blk0=0, prev=attn,
                      kv_per_step=_tile(C_HEADS, LATENT_MLA_HEADS_PER_STEP, 1), name="odd_attention_latent")
    conv = _gated_conv(proj, 0, 1, 2, conv_w, rows_p, seq_p, seq_s)
    w_out_bf = w_out.astype(BF16)
    x_new = _out_proj(attn, conv, w_out_bf[:out_cols], w_out_bf[out_cols:], x, mod, layer, 2, rows_p, seq_s,
                      name="odd_out_proj")
    ckv_state = ckvn[:rows_p].reshape(n_bp, seq_p, KV_LORA)
    kr_state = proj[:rows_p, c_kr:c_kr + QK_ROPE].reshape(n_bp, seq_p, QK_ROPE)
    return x_new, ckv_state, kr_state


def kernel(x_prompt, x_sample, cache_attn_k, cache_attn_v, cache_mla_ckv, cache_mla_krope, c, c_ctx,
           norm1_w, norm2_w, w_ada, b_ada,
           even_w_in, even_q_norm, even_k_norm, even_w_out,
           odd_w_in, odd_q_a_norm, odd_kv_a_norm, odd_w_uq, odd_w_ukv, odd_conv_w, odd_w_out,
           router_w, router_b, w_gate, b_gate, w_up, b_up, w_down, b_down, final_norm_w):
    n_bp, seq_p, d = x_prompt.shape
    n_bs, seq_s, _ = x_sample.shape
    depth = w_ada.shape[0]
    rows_p = n_bp * seq_p
    dims = (rows_p, seq_p, n_bp, seq_s, n_bs)
    n_groups = 1 + n_bs
    g8 = -(-n_groups // SUBLANE) * SUBLANE
    cond = jnp.concatenate([c_ctx[None, :], c, jnp.zeros((g8 - n_groups, d), F32)], axis=0)
    ada = _ada_modulation(cond, w_ada, b_ada)
    mod = ada[:, :n_groups].reshape(depth, n_groups, N_MOD, d).transpose(0, 2, 1, 3)[:, :, :, None, :]
    x = jnp.concatenate([x_prompt.reshape(rows_p, d), x_sample.reshape(n_bs * seq_s, d)], axis=0)
    st_k, st_v, st_ckv, st_kr = [], [], [], []
    for l in range(depth):
        j = l // 2
        if l % 2 == 0:
            x, ks, vs = _even_mixer(x, mod, l, norm1_w[l], even_w_in[j], even_q_norm[j], even_k_norm[j],
                                    even_w_out[j], cache_attn_k[:, j], cache_attn_v[:, j], dims)
            st_k.append(ks)
            st_v.append(vs)
        else:
            x, cs, rs = _odd_mixer(x, mod, l, norm1_w[l], odd_w_in[j], odd_q_a_norm[j], odd_kv_a_norm[j],
                                   odd_w_uq[j], odd_w_ukv[j], odd_conv_w[j], odd_w_out[j],
                                   cache_mla_ckv[:, j], cache_mla_krope[:, j], dims)
            st_ckv.append(cs)
            st_kr.append(rs)
        x = _moe_layer(x, norm2_w[l], mod, l, router_w[l], router_b[l], w_gate, b_gate, w_up, b_up,
                       w_down, b_down, final_norm_w, rows_p, seq_s, final_norm=(l == depth - 1))
    y_ctx, y_lat = x
    y_prompt = y_ctx.reshape(n_bp, seq_p, d)
    y_sample = y_lat.reshape(n_bs, seq_s, d)
    return (y_prompt, y_sample, jnp.stack(st_k, axis=1), jnp.stack(st_v, axis=1),
            jnp.stack(st_ckv, axis=1), jnp.stack(st_kr, axis=1))
```
